```python
import jax, jax.numpy as jnp
from jax import lax
import numpy as np

D_MODEL = 2048
BATCH = 4
SEQ = 4096
DEPTH = 1

CHUNK = 64
D_MIX = D_MODEL
GMLP_WIDTH = D_MIX // 2
GMLP_HEADS = 8
GMLP_HEAD_DIM = GMLP_WIDTH // GMLP_HEADS
GMLP_BLOCK = 128
CONV_CH = D_MIX - GMLP_WIDTH
CONV_GROUPS = 8
CONV_K = 3
D_IN_PROJ = 2 * GMLP_WIDTH + 3 * CONV_CH
N_GROUPS = 4
EXPERTS_PER_GROUP = 8
N_EXPERTS = N_GROUPS * EXPERTS_PER_GROUP
TOP_K_IN_GROUP = 2
D_EXPERT = 512
MOE_BLOCK = 128
EPS = 1e-6

kernel_name = "hymba_gmlp_shortconv_hiermoe_block"


def rmsnorm(x, g):
    xf = x.astype(jnp.float32)
    y = xf * lax.rsqrt(jnp.mean(xf * xf, axis=-1, keepdims=True) + EPS)
    return (y * g.astype(jnp.float32)).astype(x.dtype)


def gmlp_mixer(u, v, v_norm_g, ws, bs):
    b_, s_, _ = v.shape
    nb = s_ // GMLP_BLOCK
    v = rmsnorm(v.reshape(b_, s_, GMLP_HEADS, GMLP_HEAD_DIM), v_norm_g.reshape(GMLP_HEADS, GMLP_HEAD_DIM))
    v = v.reshape(b_, nb, GMLP_BLOCK, GMLP_HEADS, GMLP_HEAD_DIM)
    pos = jnp.arange(GMLP_BLOCK)
    mask = (pos[:, None] // CHUNK) >= (pos[None, :] // CHUNK)
    w = jnp.where(mask[None], ws, 0).astype(v.dtype)
    mixed = jnp.einsum('hij,bnjhd->bnihd', w, v) + bs.T.astype(v.dtype)[None, None, :, :, None]
    return u * mixed.reshape(b_, s_, GMLP_WIDTH)


def conv_mixer(b_gate, c_gate, hv, conv_w):
    z = c_gate * hv
    s_ = z.shape[1]
    zp = jnp.pad(z, ((0, 0), (CONV_K - 1, 0), (0, 0)))
    conv = sum(conv_w[k] * zp[:, k:k + s_, :] for k in range(CONV_K))
    return b_gate * conv


def hier_moe(x, wgr, bgr, wer, ber, w_gate, w_up, w_down):
    t_ = x.shape[0]
    d_ = x.shape[1]
    p_g = jax.nn.softmax((x @ wgr + bgr).astype(jnp.float32), axis=-1)
    pg_top, g_idx = lax.top_k(p_g, 1)
    logits_e = (x @ wer + ber).astype(jnp.float32).reshape(t_, N_GROUPS, EXPERTS_PER_GROUP)
    le = jnp.take_along_axis(logits_e, g_idx[:, :, None], axis=1)[:, 0]
    q = jax.nn.softmax(le, axis=-1)
    q_top, e_local = lax.top_k(q, TOP_K_IN_GROUP)
    q_top = q_top / jnp.sum(q_top, axis=-1, keepdims=True)
    gate = pg_top * q_top
    expert = g_idx * EXPERTS_PER_GROUP + e_local

    n_assign = t_ * TOP_K_IN_GROUP
    e_flat = expert.reshape(n_assign)
    tok_flat = jnp.repeat(jnp.arange(t_, dtype=jnp.int32), TOP_K_IN_GROUP)
    w_flat = gate.reshape(n_assign)
    order = jnp.argsort(e_flat)
    e_s = e_flat[order]
    tok_s = tok_flat[order]
    w_s = w_flat[order]
    counts = jnp.bincount(e_flat, length=N_EXPERTS)
    start = jnp.cumsum(counts) - counts
    padded = (counts + MOE_BLOCK - 1) // MOE_BLOCK * MOE_BLOCK
    pad_end = jnp.cumsum(padded)
    pad_start = pad_end - padded
    dest = pad_start[e_s] + jnp.arange(n_assign) - start[e_s]
    n_blocks = -(-n_assign // MOE_BLOCK) + N_EXPERTS
    rows = n_blocks * MOE_BLOCK
    buf_tok = jnp.zeros((rows,), jnp.int32).at[dest].set(tok_s)
    buf_w = jnp.zeros((rows,), jnp.float32).at[dest].set(w_s)
    blk_expert = jnp.minimum(
        jnp.searchsorted(pad_end, jnp.arange(n_blocks) * MOE_BLOCK, side='right'), N_EXPERTS - 1)

    def run_block(args):
        toks, e = args
        xb = x[toks]
        hb = jax.nn.silu(xb @ w_gate[e]) * (xb @ w_up[e])
        return hb @ w_down[e]

    out = lax.map(run_block, (buf_tok.reshape(n_blocks, MOE_BLOCK), blk_expert))
    out = out.reshape(rows, d_) * buf_w[:, None].astype(x.dtype)
    return jnp.zeros_like(x).at[buf_tok].add(out)


def setup_inputs(seed: int = 0) -> dict:
    key = jax.random.key(seed)
    ks = jax.random.split(key, 24)
    L = DEPTH

    def nrm(k, shape, scale):
        return jax.random.normal(k, shape, jnp.float32) * scale

    def gain(k, shape):
        return 1.0 + nrm(k, shape, 0.02)

    return {
        'x': nrm(ks[0], (BATCH, SEQ, D_MODEL), 1.0),
        'norm_mix_g': gain(ks[1], (L, D_MODEL)),
        'w_in': nrm(ks[2], (L, D_MODEL, D_IN_PROJ), D_MODEL ** -0.5),
        'gmlp_v_norm_g': gain(ks[3], (L, GMLP_WIDTH)),
        'gmlp_ws': nrm(ks[4], (L, GMLP_HEADS, GMLP_BLOCK, GMLP_BLOCK), GMLP_BLOCK ** -0.5),
        'gmlp_bs': gain(ks[5], (L, GMLP_HEADS, GMLP_BLOCK)),
        'conv_w': nrm(ks[6], (L, CONV_K, CONV_CH), CONV_K ** -0.5),
        'out_norm_gmlp_g': gain(ks[7], (L, GMLP_WIDTH)),
        'out_norm_conv_g': gain(ks[8], (L, CONV_CH)),
        'w_out': nrm(ks[9], (L, D_MIX, D_MODEL), D_MIX ** -0.5),
        'norm_ffn_g': gain(ks[10], (L, D_MODEL)),
        'router_group_w': nrm(ks[11], (L, D_MODEL, N_GROUPS), D_MODEL ** -0.5),
        'router_group_b': nrm(ks[12], (L, N_GROUPS), 0.01),
        'router_expert_w': nrm(ks[13], (L, D_MODEL, N_EXPERTS), D_MODEL ** -0.5),
        'router_expert_b': nrm(ks[14], (L, N_EXPERTS), 0.01),
        'expert_w_gate': nrm(ks[15], (L, N_EXPERTS, D_MODEL, D_EXPERT), D_MODEL ** -0.5),
        'expert_w_up': nrm(ks[16], (L, N_EXPERTS, D_MODEL, D_EXPERT), D_MODEL ** -0.5),
        'expert_w_down': nrm(ks[17], (L, N_EXPERTS, D_EXPERT, D_MODEL), D_EXPERT ** -0.5),
        'norm_final_g': gain(ks[18], (D_MODEL,)),
    }


def reference(x, norm_mix_g, w_in, gmlp_v_norm_g, gmlp_ws, gmlp_bs, conv_w, out_norm_gmlp_g,
              out_norm_conv_g, w_out, norm_ffn_g, router_group_w, router_group_b, router_expert_w,
              router_expert_b, expert_w_gate, expert_w_up, expert_w_down, norm_final_g):
    b_, s_, d_ = x.shape
    splits = [GMLP_WIDTH, 2 * GMLP_WIDTH, 2 * GMLP_WIDTH + CONV_CH, 2 * GMLP_WIDTH + 2 * CONV_CH]
    h = x
    for l in range(DEPTH):
        xn = rmsnorm(h, norm_mix_g[l])
        proj = xn @ w_in[l]
        u, v, bg, cg, hv = jnp.split(proj, splits, axis=-1)
        ya = gmlp_mixer(jax.nn.gelu(u), jax.nn.gelu(v), gmlp_v_norm_g[l], gmlp_ws[l], gmlp_bs[l])
        yb = conv_mixer(bg, cg, hv, conv_w[l])
        y = jnp.concatenate([rmsnorm(ya, out_norm_gmlp_g[l]), rmsnorm(yb, out_norm_conv_g[l])], axis=-1)
        h = h + y @ w_out[l]
        hn = rmsnorm(h, norm_ffn_g[l]).reshape(b_ * s_, d_)
        m = hier_moe(hn, router_group_w[l], router_group_b[l], router_expert_w[l], router_expert_b[l],
                     expert_w_gate[l], expert_w_up[l], expert_w_down[l])
        h = h + m.reshape(b_, s_, d_)
    return rmsnorm(h, norm_final_g)
```

```python
import functools

import jax
import jax.numpy as jnp
from jax import lax
from jax.experimental import pallas as pl
from jax.experimental.pallas import tpu as pltpu

F32 = jnp.float32
BF16 = jnp.bfloat16
I32 = jnp.int32

D_MODEL = 2048
CHUNK = 64
GMLP_WIDTH = 1024
GMLP_HEADS = 8
HEAD_DIM = 128
GMLP_BLOCK = 128
CONV_CH = 1024
CONV_K = 3
N_GROUPS = 4
EXPERTS_PER_GROUP = 8
N_EXPERTS = 32
D_EXPERT = 512
EPS = 1e-6

LANES = 128
SUBLANES = 8

MIX_ROWS = 512
MIX_COLS = 256
OUT_ROWS = 512
PERM_ROWS = 512
EXP_ROWS = 256
COMB_ROWS = 256
ZERO_ROWS = 128
GROUP_LANE0 = N_EXPERTS
MASKED = -1e30
VMEM_LIMIT = 56 * 1024 * 1024


def _rms_scale(sumsq, width):
    return lax.rsqrt(sumsq / width + EPS)


def _mixer_proj_body(seq_len, x_ref, gmix_ref, w_ref, vng_ref, ws_ref, bsb_ref, cw_ref,
                     ya_ref, yb_ref, sa_ref, sb_ref, xn_scr, carry_scr):
    i = pl.program_id(0)
    c = pl.program_id(1)
    tm = x_ref.shape[0]
    cw = ya_ref.shape[1]
    heads_per_step = cw // HEAD_DIM

    @pl.when(c == 0)
    def _():
        x = x_ref[...]
        ms = jnp.mean(x * x, axis=-1, keepdims=True)
        xn_scr[...] = (x * lax.rsqrt(ms + EPS) * gmix_ref[...]).astype(BF16)
        sa_ref[...] = jnp.zeros_like(sa_ref)
        sb_ref[...] = jnp.zeros_like(sb_ref)

    proj = jnp.dot(xn_scr[...], w_ref[0], preferred_element_type=F32)
    u = jax.nn.gelu(proj[:, 0:cw])
    v = jax.nn.gelu(proj[:, cw:2 * cw])
    bg = proj[:, 2 * cw:3 * cw]
    cg = proj[:, 3 * cw:4 * cw]
    hv = proj[:, 4 * cw:5 * cw]

    pos_i = lax.broadcasted_iota(I32, (GMLP_BLOCK, GMLP_BLOCK), 0)
    pos_j = lax.broadcasted_iota(I32, (GMLP_BLOCK, GMLP_BLOCK), 1)
    chunk_shift = CHUNK.bit_length() - 1
    causal = (pos_i >> chunk_shift) >= (pos_j >> chunk_shift)
    vng = vng_ref[0]
    ya_heads = []
    for j in range(heads_per_step):
        head = c * heads_per_step + j
        lanes = slice(j * HEAD_DIM, (j + 1) * HEAD_DIM)
        vj = v[:, lanes]
        ms = jnp.mean(vj * vj, axis=-1, keepdims=True)
        vn = (vj * lax.rsqrt(ms + EPS) * vng[:, lanes]).astype(BF16)
        w_mix = jnp.where(causal, ws_ref[head], 0.0).astype(BF16)
        bias = bsb_ref[head]
        mixed = [
            jnp.dot(w_mix, vn[p * GMLP_BLOCK:(p + 1) * GMLP_BLOCK, :],
                    preferred_element_type=F32) + bias
            for p in range(tm // GMLP_BLOCK)
        ]
        ya_heads.append(u[:, lanes] * jnp.concatenate(mixed, axis=0))
    ya = jnp.concatenate(ya_heads, axis=1) if heads_per_step > 1 else ya_heads[0]

    z = cg * hv
    prev = carry_scr[c]
    seq_start = (i * tm) % seq_len == 0
    prev = jnp.where(seq_start, 0.0, prev)
    row = lax.broadcasted_iota(I32, (tm, cw), 0)
    z1 = jnp.where(row == 0, prev[SUBLANES - 1:SUBLANES, :], pltpu.roll(z, 1, axis=0))
    z2 = jnp.where(row == 0, prev[SUBLANES - 2:SUBLANES - 1, :],
                   jnp.where(row == 1, prev[SUBLANES - 1:SUBLANES, :],
                             pltpu.roll(z, 2, axis=0)))
    taps = cw_ref[0]
    conv = taps[0:1, :] * z2 + taps[1:2, :] * z1 + taps[2:3, :] * z
    yb = bg * conv
    carry_scr[c] = z[tm - SUBLANES:tm, :]

    ya_ref[...] = ya.astype(BF16)
    yb_ref[...] = yb.astype(BF16)
    ya2 = ya * ya
    yb2 = yb * yb
    sa_ref[...] += sum(ya2[:, k * LANES:(k + 1) * LANES] for k in range(cw // LANES))
    sb_ref[...] += sum(yb2[:, k * LANES:(k + 1) * LANES] for k in range(cw // LANES))


def _mixer_proj(x2, gmix, w_in_r, vng, ws, bsb, conv_taps, seq_len):
    t = x2.shape[0]
    n_groups = w_in_r.shape[0]
    cw = MIX_COLS
    tm = MIX_ROWS
    grid = (t // tm, n_groups)
    return pl.pallas_call(
        functools.partial(_mixer_proj_body, seq_len),
        name="mixer_proj",
        grid=grid,
        in_specs=[
            pl.BlockSpec((tm, D_MODEL), lambda i, c: (i, 0)),
            pl.BlockSpec((1, D_MODEL), lambda i, c: (0, 0)),
            pl.BlockSpec((1, D_MODEL, 5 * cw), lambda i, c: (c, 0, 0)),
            pl.BlockSpec((1, 1, cw), lambda i, c: (c, 0, 0)),
            pl.BlockSpec((GMLP_HEADS, GMLP_BLOCK, GMLP_BLOCK), lambda i, c: (0, 0, 0)),
            pl.BlockSpec((GMLP_HEADS, GMLP_BLOCK, HEAD_DIM), lambda i, c: (0, 0, 0)),
            pl.BlockSpec((1, CONV_K, cw), lambda i, c: (c, 0, 0)),
        ],
        out_specs=[
            pl.BlockSpec((tm, cw), lambda i, c: (i, c)),
            pl.BlockSpec((tm, cw), lambda i, c: (i, c)),
            pl.BlockSpec((tm, LANES), lambda i, c: (i, 0)),
            pl.BlockSpec((tm, LANES), lambda i, c: (i, 0)),
        ],
        out_shape=[
            jax.ShapeDtypeStruct((t, GMLP_WIDTH), BF16),
            jax.ShapeDtypeStruct((t, CONV_CH), BF16),
            jax.ShapeDtypeStruct((t, LANES), F32),
            jax.ShapeDtypeStruct((t, LANES), F32),
        ],
        scratch_shapes=[
            pltpu.VMEM((tm, D_MODEL), BF16),
            pltpu.VMEM((n_groups, SUBLANES, cw), F32),
        ],
        compiler_params=pltpu.CompilerParams(
            dimension_semantics=("arbitrary", "arbitrary"),
            vmem_limit_bytes=VMEM_LIMIT),
    )(x2, gmix, w_in_r, vng, ws, bsb, conv_taps)


def _out_route_body(ya_ref, yb_ref, sa_ref, sb_ref, x_ref, ga_ref, gb_ref, wout_ref, gffn_ref,
                    wr_ref, br_ref, h_ref, hn_ref, meta_ref, gcol_ref, cnt_ref, run_scr):
    i = pl.program_id(0)
    tm = x_ref.shape[0]

    @pl.when(i == 0)
    def _():
        run_scr[...] = jnp.zeros_like(run_scr)

    ra = _rms_scale(jnp.sum(sa_ref[...], axis=-1, keepdims=True), GMLP_WIDTH)
    rb = _rms_scale(jnp.sum(sb_ref[...], axis=-1, keepdims=True), CONV_CH)
    yna = (ya_ref[...].astype(F32) * ra * ga_ref[...]).astype(BF16)
    ynb = (yb_ref[...].astype(F32) * rb * gb_ref[...]).astype(BF16)
    yn = jnp.concatenate([yna, ynb], axis=1)
    h = x_ref[...] + jnp.dot(yn, wout_ref[...], preferred_element_type=F32)
    h_ref[...] = h

    ms = jnp.mean(h * h, axis=-1, keepdims=True)
    hn = h * lax.rsqrt(ms + EPS) * gffn_ref[...]
    hn_ref[...] = hn
    logits = jnp.dot(hn.astype(BF16), wr_ref[...], preferred_element_type=F32) + br_ref[...]

    lane = lax.broadcasted_iota(I32, (tm, LANES), 1)
    lane_f = lane.astype(F32)
    big = float(LANES)

    is_g = (lane >= GROUP_LANE0) & (lane < GROUP_LANE0 + N_GROUPS)
    lg = jnp.where(is_g, logits, MASKED)
    eg = jnp.where(is_g, jnp.exp(lg - jnp.max(lg, axis=-1, keepdims=True)), 0.0)
    pg = eg / jnp.sum(eg, axis=-1, keepdims=True)
    pg_top = jnp.max(pg, axis=-1, keepdims=True)
    g_idx = jnp.min(jnp.where(is_g & (pg == pg_top), lane_f - GROUP_LANE0, big),
                    axis=-1, keepdims=True).astype(I32)

    group_shift = EXPERTS_PER_GROUP.bit_length() - 1
    in_grp = (lane < N_EXPERTS) & ((lane >> group_shift) == g_idx)
    le = jnp.where(in_grp, logits, MASKED)
    ee = jnp.where(in_grp, jnp.exp(le - jnp.max(le, axis=-1, keepdims=True)), 0.0)
    q = ee / jnp.sum(ee, axis=-1, keepdims=True)
    q1 = jnp.max(jnp.where(in_grp, q, -1.0), axis=-1, keepdims=True)
    e1 = jnp.min(jnp.where(in_grp & (q == q1), lane_f, big), axis=-1, keepdims=True)
    rest = in_grp & (lane_f != e1)
    q2 = jnp.max(jnp.where(rest, q, -1.0), axis=-1, keepdims=True)
    e2 = jnp.min(jnp.where(rest & (q == q2), lane_f, big), axis=-1, keepdims=True)
    qs = q1 + q2
    gate1 = pg_top * (q1 / qs)
    gate2 = pg_top * (q2 / qs)

    is1 = lane_f == e1
    is2 = lane_f == e2
    sel = jnp.where(is1 | is2, 1.0, 0.0)
    t_i = lax.broadcasted_iota(I32, (tm, tm), 0)
    t_j = lax.broadcasted_iota(I32, (tm, tm), 1)
    before = jnp.where(t_j < t_i, 1.0, 0.0).astype(BF16)
    cum = jnp.dot(before, sel.astype(BF16), preferred_element_type=F32) + run_scr[0:1, :]
    rank1 = jnp.sum(jnp.where(is1, cum, 0.0), axis=-1, keepdims=True)
    rank2 = jnp.sum(jnp.where(is2, cum, 0.0), axis=-1, keepdims=True)
    new_run = run_scr[0:1, :] + jnp.sum(sel, axis=0, keepdims=True)
    run_scr[...] = jnp.broadcast_to(new_run, run_scr.shape)
    cnt_ref[...] = jnp.broadcast_to(new_run, cnt_ref.shape)

    packed = jnp.where(lane == 0, e1, jnp.where(lane == 1, e2,
             jnp.where(lane == 2, rank1, jnp.where(lane == 3, rank2, 0.0))))
    meta_ref[...] = packed.T[0:SUBLANES, :]
    gcol_ref[...] = jnp.where(lane == 0, gate1, jnp.where(lane == 1, gate2, 0.0))


def _out_route(ya, yb, sa, sb, x2, ga, gb, w_out, gffn, wr, br):
    t = x2.shape[0]
    tm = OUT_ROWS
    const = lambda i: (0, 0)
    return pl.pallas_call(
        _out_route_body,
        name="out_route",
        grid=(t // tm,),
        in_specs=[
            pl.BlockSpec((tm, GMLP_WIDTH), lambda i: (i, 0)),
            pl.BlockSpec((tm, CONV_CH), lambda i: (i, 0)),
            pl.BlockSpec((tm, LANES), lambda i: (i, 0)),
            pl.BlockSpec((tm, LANES), lambda i: (i, 0)),
            pl.BlockSpec((tm, D_MODEL), lambda i: (i, 0)),
            pl.BlockSpec((1, GMLP_WIDTH), const),
            pl.BlockSpec((1, CONV_CH), const),
            pl.BlockSpec((D_MODEL, D_MODEL), const, pipeline_mode=pl.Buffered(1)),
            pl.BlockSpec((1, D_MODEL), const),
            pl.BlockSpec((D_MODEL, LANES), const),
            pl.BlockSpec((1, LANES), const),
        ],
        out_specs=[
            pl.BlockSpec((tm, D_MODEL), lambda i: (i, 0)),
            pl.BlockSpec((tm, D_MODEL), lambda i: (i, 0)),
            pl.BlockSpec((SUBLANES, tm), lambda i: (0, i)),
            pl.BlockSpec((tm, LANES), lambda i: (i, 0)),
            pl.BlockSpec((SUBLANES, LANES), const),
        ],
        out_shape=[
            jax.ShapeDtypeStruct((t, D_MODEL), F32),
            jax.ShapeDtypeStruct((t, D_MODEL), F32),
            jax.ShapeDtypeStruct((SUBLANES, t), F32),
            jax.ShapeDtypeStruct((t, LANES), F32),
            jax.ShapeDtypeStruct((SUBLANES, LANES), F32),
        ],
        scratch_shapes=[pltpu.VMEM((SUBLANES, LANES), F32)],
        compiler_params=pltpu.CompilerParams(
            dimension_semantics=("arbitrary",),
            vmem_limit_bytes=VMEM_LIMIT),
    )(ya, yb, sa, sb, x2, ga, gb, w_out, gffn, wr, br)


TAB_BLK_EXPERT = 0
TAB_PAD_START = 1
TAB_PAD_LEN = 2
TAB_N_BLOCKS = 3
TAB_LANES = 256


def _lane_cumsum(v, lane):
    shift = 1
    while shift < N_EXPERTS:
        v = v + jnp.where(lane >= shift, pltpu.roll(v, shift, axis=1), 0.0)
        shift *= 2
    return v


def _route_table_body(meta_ref, cnt_ref, pos_ref, tab_ref):
    lane = lax.broadcasted_iota(I32, (1, LANES), 1)
    cnt = jnp.where(lane < N_EXPERTS, cnt_ref[0:1, :], 0.0)
    padded = jnp.ceil(cnt / EXP_ROWS) * EXP_ROWS
    seg_end = _lane_cumsum(padded, lane)
    seg_start = seg_end - padded

    e1 = meta_ref[0:1, :]
    e2 = meta_ref[1:2, :]
    p1 = meta_ref[2:3, :]
    p2 = meta_ref[3:4, :]
    blk_lane = lax.broadcasted_iota(I32, (1, TAB_LANES), 1)
    blk_start = (blk_lane * EXP_ROWS).astype(F32)
    blk_expert = jnp.zeros((1, TAB_LANES), F32)
    for e in range(N_EXPERTS):
        start_e = jnp.sum(jnp.where(lane == e, seg_start, 0.0), axis=-1, keepdims=True)
        end_e = jnp.sum(jnp.where(lane == e, seg_end, 0.0), axis=-1, keepdims=True)
        p1 = p1 + jnp.where(e1 == e, start_e, 0.0)
        p2 = p2 + jnp.where(e2 == e, start_e, 0.0)
        blk_expert = blk_expert + jnp.where(end_e <= blk_start, 1.0, 0.0)
    pos_ref[0:1, :] = p1.astype(I32)
    pos_ref[1:2, :] = p2.astype(I32)

    total = jnp.sum(jnp.where(lane == N_EXPERTS - 1, seg_end, 0.0), axis=-1, keepdims=True)
    widen = lambda v: jnp.concatenate([v, jnp.zeros_like(v)], axis=1)
    tab_ref[...] = jnp.zeros_like(tab_ref)
    tab_ref[TAB_BLK_EXPERT:TAB_BLK_EXPERT + 1, :] = jnp.minimum(
        blk_expert, N_EXPERTS - 1.0).astype(I32)
    tab_ref[TAB_PAD_START:TAB_PAD_START + 1, :] = widen(seg_start + cnt).astype(I32)
    tab_ref[TAB_PAD_LEN:TAB_PAD_LEN + 1, :] = widen(padded - cnt).astype(I32)
    tab_ref[TAB_N_BLOCKS:TAB_N_BLOCKS + 1, :] = jnp.broadcast_to(
        total / EXP_ROWS, (1, TAB_LANES)).astype(I32)


def _route_table(meta, cnt):
    t = meta.shape[1]
    return pl.pallas_call(
        _route_table_body,
        name="route_table",
        out_shape=[
            jax.ShapeDtypeStruct((2, t), I32),
            jax.ShapeDtypeStruct((SUBLANES, TAB_LANES), I32),
        ],
    )(meta, cnt)


def _zero_fill_copies(tab_ref, zero_scr, xs_ref, sem, e):
    start = tab_ref[TAB_PAD_START, e]
    length = tab_ref[TAB_PAD_LEN, e]
    head = length & (SUBLANES - 1)
    copies = []
    for j in range(SUBLANES - 1):
        copy = pltpu.make_async_copy(
            zero_scr.at[pl.ds(0, 1)], xs_ref.at[pl.ds(start + j, 1)], sem)
        copies.append((j < head, copy))
    body_start = start + head
    body = length - head
    size = ZERO_ROWS
    while size >= SUBLANES:
        offset = pl.multiple_of(body_start + (body & ~(2 * size - 1)), SUBLANES)
        copy = pltpu.make_async_copy(
            zero_scr.at[pl.ds(0, size)], xs_ref.at[pl.ds(offset, size)], sem)
        copies.append(((body & size) != 0, copy))
        size //= 2
    return copies


def _permute_rows_body(tab_ref, pos_ref, hn_ref, xs_ref, zero_scr, sem, zsem):
    i = pl.program_id(0)
    tm = hn_ref.shape[0]

    @pl.when(i == 0)
    def _():
        zero_scr[...] = jnp.zeros_like(zero_scr)

        def start_fill(e, carry):
            for needed, copy in _zero_fill_copies(tab_ref, zero_scr, xs_ref, zsem, e):
                pl.when(needed)(copy.start)
            return carry

        def wait_fill(e, carry):
            for needed, copy in _zero_fill_copies(tab_ref, zero_scr, xs_ref, zsem, e):
                pl.when(needed)(copy.wait)
            return carry

        def tail_copies(b):
            return [
                pltpu.make_async_copy(
                    zero_scr,
                    xs_ref.at[pl.ds(pl.multiple_of(b * EXP_ROWS + part * ZERO_ROWS, ZERO_ROWS),
                                    ZERO_ROWS)],
                    zsem)
                for part in range(EXP_ROWS // ZERO_ROWS)
            ]

        def start_tail(b, carry):
            for copy in tail_copies(b):
                copy.start()
            return carry

        def wait_tail(b, carry):
            for copy in tail_copies(b):
                copy.wait()
            return carry

        n_blocks = tab_ref[TAB_N_BLOCKS, 0]
        total_blocks = xs_ref.shape[0] // EXP_ROWS
        lax.fori_loop(0, N_EXPERTS, start_fill, 0)
        lax.fori_loop(n_blocks, total_blocks, start_tail, 0)
        lax.fori_loop(0, N_EXPERTS, wait_fill, 0)
        lax.fori_loop(n_blocks, total_blocks, wait_tail, 0)

    def issue(r, carry):
        for k in range(2):
            pltpu.make_async_copy(
                hn_ref.at[pl.ds(r, 1)], xs_ref.at[pl.ds(pos_ref[k, r], 1)], sem).start()
        return carry

    lax.fori_loop(0, tm, issue, 0, unroll=8)
    for _ in range(2):
        pltpu.make_async_copy(hn_ref, xs_ref.at[pl.ds(0, tm)], sem).wait()


def _permute_rows(tab, pos, hn, n_rows):
    t = hn.shape[0]
    tm = PERM_ROWS
    return pl.pallas_call(
        _permute_rows_body,
        name="permute_rows",
        grid_spec=pltpu.PrefetchScalarGridSpec(
            num_scalar_prefetch=1,
            grid=(t // tm,),
            in_specs=[
                pl.BlockSpec((2, tm), lambda i, tab: (0, i), memory_space=pltpu.SMEM),
                pl.BlockSpec((tm, D_MODEL), lambda i, tab: (i, 0)),
            ],
            out_specs=pl.BlockSpec(memory_space=pl.ANY),
            scratch_shapes=[
                pltpu.VMEM((ZERO_ROWS, D_MODEL), F32),
                pltpu.SemaphoreType.DMA,
                pltpu.SemaphoreType.DMA,
            ],
        ),
        out_shape=jax.ShapeDtypeStruct((n_rows, D_MODEL), F32),
        compiler_params=pltpu.CompilerParams(
            dimension_semantics=("arbitrary",),
            vmem_limit_bytes=VMEM_LIMIT),
    )(tab, pos, hn)


def _expert_mlp_body(tab_ref, xs_ref, wg_ref, wu_ref, wd_ref, ys_ref, wg_scr, wu_scr, wd_scr):
    b = pl.program_id(0)
    n_blocks = tab_ref[TAB_N_BLOCKS, 0]
    expert = tab_ref[TAB_BLK_EXPERT, b]
    prev_expert = tab_ref[TAB_BLK_EXPERT, jnp.maximum(b - 1, 0)]
    in_use = b < n_blocks

    @pl.when(in_use & ((b == 0) | (expert != prev_expert)))
    def _():
        wg_scr[...] = wg_ref[0].astype(BF16)
        wu_scr[...] = wu_ref[0].astype(BF16)
        wd_scr[...] = wd_ref[0].astype(BF16)

    @pl.when(in_use)
    def _():
        x = xs_ref[...].astype(BF16)
        gate = jnp.dot(x, wg_scr[...], preferred_element_type=F32)
        up = jnp.dot(x, wu_scr[...], preferred_element_type=F32)
        hidden = (jax.nn.silu(gate) * up).astype(BF16)
        ys_ref[...] = jnp.dot(hidden, wd_scr[...], preferred_element_type=F32)

    @pl.when(jnp.logical_not(in_use))
    def _():
        ys_ref[...] = jnp.zeros_like(ys_ref)


def _expert_mlp(tab, xs, w_gate, w_up, w_down):
    n_rows = xs.shape[0]
    bm = EXP_ROWS

    def block_of(b, tab):
        return jnp.minimum(b, tab[TAB_N_BLOCKS, 0] - 1)

    def expert_of(b, tab):
        return tab[TAB_BLK_EXPERT, block_of(b, tab)]

    return pl.pallas_call(
        _expert_mlp_body,
        name="expert_mlp",
        grid_spec=pltpu.PrefetchScalarGridSpec(
            num_scalar_prefetch=1,
            grid=(n_rows // bm,),
            in_specs=[
                pl.BlockSpec((bm, D_MODEL), lambda b, tab: (block_of(b, tab), 0)),
                pl.BlockSpec((1, D_MODEL, D_EXPERT), lambda b, tab: (expert_of(b, tab), 0, 0)),
                pl.BlockSpec((1, D_MODEL, D_EXPERT), lambda b, tab: (expert_of(b, tab), 0, 0)),
                pl.BlockSpec((1, D_EXPERT, D_MODEL), lambda b, tab: (expert_of(b, tab), 0, 0)),
            ],
            out_specs=pl.BlockSpec((bm, D_MODEL), lambda b, tab: (b, 0)),
            scratch_shapes=[
                pltpu.VMEM((D_MODEL, D_EXPERT), BF16),
                pltpu.VMEM((D_MODEL, D_EXPERT), BF16),
                pltpu.VMEM((D_EXPERT, D_MODEL), BF16),
            ],
        ),
        out_shape=jax.ShapeDtypeStruct((n_rows, D_MODEL), F32),
        compiler_params=pltpu.CompilerParams(
            dimension_semantics=("arbitrary",),
            vmem_limit_bytes=VMEM_LIMIT),
    )(tab, xs, w_gate, w_up, w_down)


def _combine_norm_body(pos_ref, h_ref, gcol_ref, gfin_ref, ys_ref, out_ref, y_scr, sem):
    tm = h_ref.shape[0]

    def issue(r, carry):
        for k in range(2):
            pltpu.make_async_copy(
                ys_ref.at[pl.ds(pos_ref[k, r], 1)], y_scr.at[k, pl.ds(r, 1)], sem).start()
        return carry

    lax.fori_loop(0, tm, issue, 0, unroll=8)
    for k in range(2):
        pltpu.make_async_copy(ys_ref.at[pl.ds(0, tm)], y_scr.at[k], sem).wait()

    gates = gcol_ref[...]
    moe = y_scr[0] * gates[:, 0:1] + y_scr[1] * gates[:, 1:2]
    h = h_ref[...] + moe
    ms = jnp.mean(h * h, axis=-1, keepdims=True)
    out_ref[...] = h * lax.rsqrt(ms + EPS) * gfin_ref[...]


def _combine_norm(pos, h, gcol, gfin, ys):
    t = h.shape[0]
    tm = COMB_ROWS
    return pl.pallas_call(
        _combine_norm_body,
        name="combine_norm",
        grid=(t // tm,),
        in_specs=[
            pl.BlockSpec((2, tm), lambda i: (0, i), memory_space=pltpu.SMEM),
            pl.BlockSpec((tm, D_MODEL), lambda i: (i, 0)),
            pl.BlockSpec((tm, LANES), lambda i: (i, 0)),
            pl.BlockSpec((1, D_MODEL), lambda i: (0, 0)),
            pl.BlockSpec(memory_space=pl.ANY),
        ],
        out_specs=pl.BlockSpec((tm, D_MODEL), lambda i: (i, 0)),
        out_shape=jax.ShapeDtypeStruct((t, D_MODEL), F32),
        scratch_shapes=[
            pltpu.VMEM((2, tm, D_MODEL), F32),
            pltpu.SemaphoreType.DMA,
        ],
        compiler_params=pltpu.CompilerParams(
            dimension_semantics=("arbitrary",),
            vmem_limit_bytes=VMEM_LIMIT),
    )(pos, h, gcol, gfin, ys)


def kernel(x, norm_mix_g, w_in, gmlp_v_norm_g, gmlp_ws, gmlp_bs, conv_w, out_norm_gmlp_g,
           out_norm_conv_g, w_out, norm_ffn_g, router_group_w, router_group_b, router_expert_w,
           router_expert_b, expert_w_gate, expert_w_up, expert_w_down, norm_final_g):
    batch, seq_len, d_model = x.shape
    t = batch * seq_len
    assert w_in.shape[0] == 1, "single-layer block"
    n_col_groups = GMLP_WIDTH // MIX_COLS
    n_rows = t * 2 + N_EXPERTS * EXP_ROWS
    x2 = x.reshape(t, d_model)

    w_in_r = (w_in[0].reshape(d_model, 5, n_col_groups, MIX_COLS)
              .transpose(2, 0, 1, 3).reshape(n_col_groups, d_model, 5 * MIX_COLS).astype(BF16))
    vng = gmlp_v_norm_g[0].reshape(n_col_groups, 1, MIX_COLS)
    bsb = jnp.broadcast_to(gmlp_bs[0][:, :, None], (GMLP_HEADS, GMLP_BLOCK, HEAD_DIM))
    taps = conv_w[0].reshape(CONV_K, n_col_groups, MIX_COLS).transpose(1, 0, 2)
    unused = LANES - N_EXPERTS - N_GROUPS
    wr = jnp.concatenate(
        [router_expert_w[0], router_group_w[0], jnp.zeros((d_model, unused), F32)],
        axis=1).astype(BF16)
    br = jnp.concatenate(
        [router_expert_b[0], router_group_b[0], jnp.zeros((unused,), F32)])[None, :]

    ya, yb, sa, sb = _mixer_proj(x2, norm_mix_g[0][None, :], w_in_r, vng, gmlp_ws[0], bsb,
                                 taps, seq_len)
    h, hn, meta, gcol, cnt = _out_route(
        ya, yb, sa, sb, x2, out_norm_gmlp_g[0][None, :], out_norm_conv_g[0][None, :],
        w_out[0].astype(BF16), norm_ffn_g[0][None, :], wr, br)
    pos, tab = _route_table(meta, cnt)
    xs = _permute_rows(tab, pos, hn, n_rows)
    ys = _expert_mlp(tab, xs, expert_w_gate[0], expert_w_up[0], expert_w_down[0])
    out = _combine_norm(pos, h, gcol, norm_final_g[None, :], ys)
    return out.reshape(batch, seq_len, d_model)
```

```python
import functools

import jax
import jax.numpy as jnp
from jax import lax
from jax.experimental import pallas as pl
from jax.experimental.pallas import tpu as pltpu

F32 = jnp.float32
BF16 = jnp.bfloat16
I32 = jnp.int32

D_MODEL = 2048
CHUNK = 64
GMLP_WIDTH = 1024
GMLP_HEADS = 8
HEAD_DIM = 128
GMLP_BLOCK = 128
CONV_CH = 1024
CONV_K = 3
N_GROUPS = 4
EXPERTS_PER_GROUP = 8
N_EXPERTS = 32
D_EXPERT = 512
EPS = 1e-6

LANES = 128
SUBLANES = 8

MIX_ROWS = 1024
MIX_COLS = 256
OUT_ROWS = 512
PERM_ROWS = 512
EXP_ROWS = 256
COMB_ROWS = 256
ZERO_ROWS = 128
GROUP_LANE0 = N_EXPERTS
MASKED = -1e30
VMEM_LIMIT = 56 * 1024 * 1024


def _rms_scale(sumsq, width):
    return lax.rsqrt(sumsq / width + EPS)


def _mixer_proj_body(seq_len, x_ref, gmix_ref, wu_ref, wv_ref, wbg_ref, wcg_ref, whv_ref,
                     vng_ref, ws_ref, bsb_ref, cw_ref,
                     ya_ref, yb_ref, sa_ref, sb_ref, xn_scr, carry_scr):
    i = pl.program_id(0)
    c = pl.program_id(1)
    tm = x_ref.shape[0]
    cw = ya_ref.shape[1]
    heads_per_step = cw // HEAD_DIM

    @pl.when(c == 0)
    def _():
        x = x_ref[...]
        ms = jnp.mean(x * x, axis=-1, keepdims=True)
        xn_scr[...] = (x * lax.rsqrt(ms + EPS) * gmix_ref[...]).astype(BF16)
        sa_ref[...] = jnp.zeros_like(sa_ref)
        sb_ref[...] = jnp.zeros_like(sb_ref)

    xn = xn_scr[...]
    project = lambda w_ref: jnp.dot(xn, w_ref[...], preferred_element_type=F32)
    u = jax.nn.gelu(project(wu_ref))
    v = jax.nn.gelu(project(wv_ref))
    bg = project(wbg_ref)
    cg = project(wcg_ref)
    hv = project(whv_ref)

    pos_i = lax.broadcasted_iota(I32, (GMLP_BLOCK, GMLP_BLOCK), 0)
    pos_j = lax.broadcasted_iota(I32, (GMLP_BLOCK, GMLP_BLOCK), 1)
    chunk_shift = CHUNK.bit_length() - 1
    causal = (pos_i >> chunk_shift) >= (pos_j >> chunk_shift)
    vng = vng_ref[0]
    ya_heads = []
    for j in range(heads_per_step):
        head = c * heads_per_step + j
        lanes = slice(j * HEAD_DIM, (j + 1) * HEAD_DIM)
        vj = v[:, lanes]
        ms = jnp.mean(vj * vj, axis=-1, keepdims=True)
        vn = (vj * lax.rsqrt(ms + EPS) * vng[:, lanes]).astype(BF16)
        w_mix = jnp.where(causal, ws_ref[head], 0.0).astype(BF16)
        bias = bsb_ref[head]
        mixed = [
            jnp.dot(w_mix, vn[p * GMLP_BLOCK:(p + 1) * GMLP_BLOCK, :],
                    preferred_element_type=F32) + bias
            for p in range(tm // GMLP_BLOCK)
        ]
        ya_heads.append(u[:, lanes] * jnp.concatenate(mixed, axis=0))
    ya = jnp.concatenate(ya_heads, axis=1) if heads_per_step > 1 else ya_heads[0]

    z = cg * hv
    prev = carry_scr[c]
    seq_start = (i * tm) % seq_len == 0
    prev = jnp.where(seq_start, 0.0, prev)
    row = lax.broadcasted_iota(I32, (tm, cw), 0)
    z1 = jnp.where(row == 0, prev[SUBLANES - 1:SUBLANES, :], pltpu.roll(z, 1, axis=0))
    z2 = jnp.where(row == 0, prev[SUBLANES - 2:SUBLANES - 1, :],
                   jnp.where(row == 1, prev[SUBLANES - 1:SUBLANES, :],
                             pltpu.roll(z, 2, axis=0)))
    taps = cw_ref[0]
    conv = taps[0:1, :] * z2 + taps[1:2, :] * z1 + taps[2:3, :] * z
    yb = bg * conv
    carry_scr[c] = z[tm - SUBLANES:tm, :]

    ya_ref[...] = ya.astype(BF16)
    yb_ref[...] = yb.astype(BF16)
    ya2 = ya * ya
    yb2 = yb * yb
    sa_ref[...] += sum(ya2[:, k * LANES:(k + 1) * LANES] for k in range(cw // LANES))
    sb_ref[...] += sum(yb2[:, k * LANES:(k + 1) * LANES] for k in range(cw // LANES))


def _mixer_proj(x2, gmix, w_in, vng, ws, bsb, conv_taps, seq_len):
    t = x2.shape[0]
    cw = MIX_COLS
    tm = MIX_ROWS
    n_groups = GMLP_WIDTH // cw
    grid = (t // tm, n_groups)
    w_part = lambda k: pl.BlockSpec((D_MODEL, cw), lambda i, c: (0, k * n_groups + c))
    return pl.pallas_call(
        functools.partial(_mixer_proj_body, seq_len),
        name="mixer_proj",
        grid=grid,
        in_specs=[
            pl.BlockSpec((tm, D_MODEL), lambda i, c: (i, 0)),
            pl.BlockSpec((1, D_MODEL), lambda i, c: (0, 0)),
            w_part(0), w_part(1), w_part(2), w_part(3), w_part(4),
            pl.BlockSpec((1, 1, cw), lambda i, c: (c, 0, 0)),
            pl.BlockSpec((GMLP_HEADS, GMLP_BLOCK, GMLP_BLOCK), lambda i, c: (0, 0, 0)),
            pl.BlockSpec((GMLP_HEADS, GMLP_BLOCK, HEAD_DIM), lambda i, c: (0, 0, 0)),
            pl.BlockSpec((1, CONV_K, cw), lambda i, c: (c, 0, 0)),
        ],
        out_specs=[
            pl.BlockSpec((tm, cw), lambda i, c: (i, c)),
            pl.BlockSpec((tm, cw), lambda i, c: (i, c)),
            pl.BlockSpec((tm, LANES), lambda i, c: (i, 0)),
            pl.BlockSpec((tm, LANES), lambda i, c: (i, 0)),
        ],
        out_shape=[
            jax.ShapeDtypeStruct((t, GMLP_WIDTH), BF16),
            jax.ShapeDtypeStruct((t, CONV_CH), BF16),
            jax.ShapeDtypeStruct((t, LANES), F32),
            jax.ShapeDtypeStruct((t, LANES), F32),
        ],
        scratch_shapes=[
            pltpu.VMEM((tm, D_MODEL), BF16),
            pltpu.VMEM((n_groups, SUBLANES, cw), F32),
        ],
        compiler_params=pltpu.CompilerParams(
            dimension_semantics=("arbitrary", "arbitrary"),
            vmem_limit_bytes=VMEM_LIMIT),
    )(x2, gmix, w_in, w_in, w_in, w_in, w_in, vng, ws, bsb, conv_taps)


def _out_route_body(ya_ref, yb_ref, sa_ref, sb_ref, x_ref, ga_ref, gb_ref, wout_ref, gffn_ref,
                    wr_ref, br_ref, h_ref, hn_ref, meta_ref, gcol_ref, cnt_ref, run_scr):
    i = pl.program_id(0)
    tm = x_ref.shape[0]

    @pl.when(i == 0)
    def _():
        run_scr[...] = jnp.zeros_like(run_scr)

    ra = _rms_scale(jnp.sum(sa_ref[...], axis=-1, keepdims=True), GMLP_WIDTH)
    rb = _rms_scale(jnp.sum(sb_ref[...], axis=-1, keepdims=True), CONV_CH)
    yna = (ya_ref[...].astype(F32) * ra * ga_ref[...]).astype(BF16)
    ynb = (yb_ref[...].astype(F32) * rb * gb_ref[...]).astype(BF16)
    yn = jnp.concatenate([yna, ynb], axis=1)
    h = x_ref[...] + jnp.dot(yn, wout_ref[...], preferred_element_type=F32)
    h_ref[...] = h

    ms = jnp.mean(h * h, axis=-1, keepdims=True)
    hn = h * lax.rsqrt(ms + EPS) * gffn_ref[...]
    hn_ref[...] = hn
    logits = jnp.dot(hn.astype(BF16), wr_ref[...], preferred_element_type=F32) + br_ref[...]

    lane = lax.broadcasted_iota(I32, (tm, LANES), 1)
    lane_f = lane.astype(F32)
    big = float(LANES)

    is_g = (lane >= GROUP_LANE0) & (lane < GROUP_LANE0 + N_GROUPS)
    lg = jnp.where(is_g, logits, MASKED)
    eg = jnp.where(is_g, jnp.exp(lg - jnp.max(lg, axis=-1, keepdims=True)), 0.0)
    pg = eg / jnp.sum(eg, axis=-1, keepdims=True)
    pg_top = jnp.max(pg, axis=-1, keepdims=True)
    g_idx = jnp.min(jnp.where(is_g & (pg == pg_top), lane_f - GROUP_LANE0, big),
                    axis=-1, keepdims=True).astype(I32)

    group_shift = EXPERTS_PER_GROUP.bit_length() - 1
    in_grp = (lane < N_EXPERTS) & ((lane >> group_shift) == g_idx)
    le = jnp.where(in_grp, logits, MASKED)
    ee = jnp.where(in_grp, jnp.exp(le - jnp.max(le, axis=-1, keepdims=True)), 0.0)
    q = ee / jnp.sum(ee, axis=-1, keepdims=True)
    q1 = jnp.max(jnp.where(in_grp, q, -1.0), axis=-1, keepdims=True)
    e1 = jnp.min(jnp.where(in_grp & (q == q1), lane_f, big), axis=-1, keepdims=True)
    rest = in_grp & (lane_f != e1)
    q2 = jnp.max(jnp.where(rest, q, -1.0), axis=-1, keepdims=True)
    e2 = jnp.min(jnp.where(rest & (q == q2), lane_f, big), axis=-1, keepdims=True)
    qs = q1 + q2
    gate1 = pg_top * (q1 / qs)
    gate2 = pg_top * (q2 / qs)

    is1 = lane_f == e1
    is2 = lane_f == e2
    sel = jnp.where(is1 | is2, 1.0, 0.0)
    t_i = lax.broadcasted_iota(I32, (tm, tm), 0)
    t_j = lax.broadcasted_iota(I32, (tm, tm), 1)
    before = jnp.where(t_j < t_i, 1.0, 0.0).astype(BF16)
    cum = jnp.dot(before, sel.astype(BF16), preferred_element_type=F32) + run_scr[0:1, :]
    rank1 = jnp.sum(jnp.where(is1, cum, 0.0), axis=-1, keepdims=True)
    rank2 = jnp.sum(jnp.where(is2, cum, 0.0), axis=-1, keepdims=True)
    new_run = run_scr[0:1, :] + jnp.sum(sel, axis=0, keepdims=True)
    run_scr[...] = jnp.broadcast_to(new_run, run_scr.shape)
    cnt_ref[...] = jnp.broadcast_to(new_run, cnt_ref.shape)

    packed = jnp.where(lane == 0, e1, jnp.where(lane == 1, e2,
             jnp.where(lane == 2, rank1, jnp.where(lane == 3, rank2, 0.0))))
    meta_ref[...] = packed.T[0:SUBLANES, :]
    gcol_ref[...] = jnp.where(lane == 0, gate1, jnp.where(lane == 1, gate2, 0.0))


def _out_route(ya, yb, sa, sb, x2, ga, gb, w_out, gffn, wr, br):
    t = x2.shape[0]
    tm = OUT_ROWS
    const = lambda i: (0, 0)
    return pl.pallas_call(
        _out_route_body,
        name="out_route",
        grid=(t // tm,),
        in_specs=[
            pl.BlockSpec((tm, GMLP_WIDTH), lambda i: (i, 0)),
            pl.BlockSpec((tm, CONV_CH), lambda i: (i, 0)),
            pl.BlockSpec((tm, LANES), lambda i: (i, 0)),
            pl.BlockSpec((tm, LANES), lambda i: (i, 0)),
            pl.BlockSpec((tm, D_MODEL), lambda i: (i, 0)),
            pl.BlockSpec((1, GMLP_WIDTH), const),
            pl.BlockSpec((1, CONV_CH), const),
            pl.BlockSpec((D_MODEL, D_MODEL), const, pipeline_mode=pl.Buffered(1)),
            pl.BlockSpec((1, D_MODEL), const),
            pl.BlockSpec((D_MODEL, LANES), const),
            pl.BlockSpec((1, LANES), const),
        ],
        out_specs=[
            pl.BlockSpec((tm, D_MODEL), lambda i: (i, 0)),
            pl.BlockSpec((tm, D_MODEL), lambda i: (i, 0)),
            pl.BlockSpec((SUBLANES, tm), lambda i: (0, i)),
            pl.BlockSpec((tm, LANES), lambda i: (i, 0)),
            pl.BlockSpec((SUBLANES, LANES), const),
        ],
        out_shape=[
            jax.ShapeDtypeStruct((t, D_MODEL), F32),
            jax.ShapeDtypeStruct((t, D_MODEL), F32),
            jax.ShapeDtypeStruct((SUBLANES, t), F32),
            jax.ShapeDtypeStruct((t, LANES), F32),
            jax.ShapeDtypeStruct((SUBLANES, LANES), F32),
        ],
        scratch_shapes=[pltpu.VMEM((SUBLANES, LANES), F32)],
        compiler_params=pltpu.CompilerParams(
            dimension_semantics=("arbitrary",),
            vmem_limit_bytes=VMEM_LIMIT),
    )(ya, yb, sa, sb, x2, ga, gb, w_out, gffn, wr, br)


TAB_BLK_EXPERT = 0
TAB_PAD_START = 1
TAB_PAD_LEN = 2
TAB_N_BLOCKS = 3
TAB_LANES = 256


def _lane_cumsum(v, lane):
    shift = 1
    while shift < N_EXPERTS:
        v = v + jnp.where(lane >= shift, pltpu.roll(v, shift, axis=1), 0.0)
        shift *= 2
    return v


def _route_table_body(meta_ref, cnt_ref, pos_ref, tab_ref):
    lane = lax.broadcasted_iota(I32, (1, LANES), 1)
    cnt = jnp.where(lane < N_EXPERTS, cnt_ref[0:1, :], 0.0)
    padded = jnp.ceil(cnt / EXP_ROWS) * EXP_ROWS
    seg_end = _lane_cumsum(padded, lane)
    seg_start = seg_end - padded

    e1 = meta_ref[0:1, :]
    e2 = meta_ref[1:2, :]
    p1 = meta_ref[2:3, :]
    p2 = meta_ref[3:4, :]
    blk_lane = lax.broadcasted_iota(I32, (1, TAB_LANES), 1)
    blk_start = (blk_lane * EXP_ROWS).astype(F32)
    blk_expert = jnp.zeros((1, TAB_LANES), F32)
    for e in range(N_EXPERTS):
        start_e = jnp.sum(jnp.where(lane == e, seg_start, 0.0), axis=-1, keepdims=True)
        end_e = jnp.sum(jnp.where(lane == e, seg_end, 0.0), axis=-1, keepdims=True)
        p1 = p1 + jnp.where(e1 == e, start_e, 0.0)
        p2 = p2 + jnp.where(e2 == e, start_e, 0.0)
        blk_expert = blk_expert + jnp.where(end_e <= blk_start, 1.0, 0.0)
    pos_ref[0:1, :] = p1.astype(I32)
    pos_ref[1:2, :] = p2.astype(I32)

    total = jnp.sum(jnp.where(lane == N_EXPERTS - 1, seg_end, 0.0), axis=-1, keepdims=True)
    widen = lambda v: jnp.concatenate([v, jnp.zeros_like(v)], axis=1)
    tab_ref[...] = jnp.zeros_like(tab_ref)
    tab_ref[TAB_BLK_EXPERT:TAB_BLK_EXPERT + 1, :] = jnp.minimum(
        blk_expert, N_EXPERTS - 1.0).astype(I32)
    tab_ref[TAB_PAD_START:TAB_PAD_START + 1, :] = widen(seg_start + cnt).astype(I32)
    tab_ref[TAB_PAD_LEN:TAB_PAD_LEN + 1, :] = widen(padded - cnt).astype(I32)
    tab_ref[TAB_N_BLOCKS:TAB_N_BLOCKS + 1, :] = jnp.broadcast_to(
        total / EXP_ROWS, (1, TAB_LANES)).astype(I32)


def _route_table(meta, cnt):
    t = meta.shape[1]
    return pl.pallas_call(
        _route_table_body,
        name="route_table",
        out_shape=[
            jax.ShapeDtypeStruct((2, t), I32),
            jax.ShapeDtypeStruct((SUBLANES, TAB_LANES), I32),
        ],
    )(meta, cnt)


def _zero_fill_copies(tab_ref, zero_scr, xs_ref, sem, e):
    start = tab_ref[TAB_PAD_START, e]
    length = tab_ref[TAB_PAD_LEN, e]
    head = length & (SUBLANES - 1)
    copies = []
    for j in range(SUBLANES - 1):
        copy = pltpu.make_async_copy(
            zero_scr.at[pl.ds(0, 1)], xs_ref.at[pl.ds(start + j, 1)], sem)
        copies.append((j < head, copy))
    body_start = start + head
    body = length - head
    size = ZERO_ROWS
    while size >= SUBLANES:
        offset = pl.multiple_of(body_start + (body & ~(2 * size - 1)), SUBLANES)
        copy = pltpu.make_async_copy(
            zero_scr.at[pl.ds(0, size)], xs_ref.at[pl.ds(offset, size)], sem)
        copies.append(((body & size) != 0, copy))
        size //= 2
    return copies


def _permute_rows_body(tab_ref, pos_ref, hn_ref, xs_ref, zero_scr, sem, zsem):
    i = pl.program_id(0)
    tm = hn_ref.shape[0]

    @pl.when(i == 0)
    def _():
        zero_scr[...] = jnp.zeros_like(zero_scr)

        def start_fill(e, carry):
            for needed, copy in _zero_fill_copies(tab_ref, zero_scr, xs_ref, zsem, e):
                pl.when(needed)(copy.start)
            return carry

        def wait_fill(e, carry):
            for needed, copy in _zero_fill_copies(tab_ref, zero_scr, xs_ref, zsem, e):
                pl.when(needed)(copy.wait)
            return carry

        def tail_copies(b):
            return [
                pltpu.make_async_copy(
                    zero_scr,
                    xs_ref.at[pl.ds(pl.multiple_of(b * EXP_ROWS + part * ZERO_ROWS, ZERO_ROWS),
                                    ZERO_ROWS)],
                    zsem)
                for part in range(EXP_ROWS // ZERO_ROWS)
            ]

        def start_tail(b, carry):
            for copy in tail_copies(b):
                copy.start()
            return carry

        def wait_tail(b, carry):
            for copy in tail_copies(b):
                copy.wait()
            return carry

        n_blocks = tab_ref[TAB_N_BLOCKS, 0]
        total_blocks = xs_ref.shape[0] // EXP_ROWS
        lax.fori_loop(0, N_EXPERTS, start_fill, 0)
        lax.fori_loop(n_blocks, total_blocks, start_tail, 0)
        lax.fori_loop(0, N_EXPERTS, wait_fill, 0)
        lax.fori_loop(n_blocks, total_blocks, wait_tail, 0)

    def issue(r, carry):
        for k in range(2):
            pltpu.make_async_copy(
                hn_ref.at[pl.ds(r, 1)], xs_ref.at[pl.ds(pos_ref[k, r], 1)], sem).start()
        return carry

    lax.fori_loop(0, tm, issue, 0, unroll=8)
    for _ in range(2):
        pltpu.make_async_copy(hn_ref, xs_ref.at[pl.ds(0, tm)], sem).wait()


def _permute_rows(tab, pos, hn, n_rows):
    t = hn.shape[0]
    tm = PERM_ROWS
    return pl.pallas_call(
        _permute_rows_body,
        name="permute_rows",
        grid_spec=pltpu.PrefetchScalarGridSpec(
            num_scalar_prefetch=1,
            grid=(t // tm,),
            in_specs=[
                pl.BlockSpec((2, tm), lambda i, tab: (0, i), memory_space=pltpu.SMEM),
                pl.BlockSpec((tm, D_MODEL), lambda i, tab: (i, 0)),
            ],
            out_specs=pl.BlockSpec(memory_space=pl.ANY),
            scratch_shapes=[
                pltpu.VMEM((ZERO_ROWS, D_MODEL), F32),
                pltpu.SemaphoreType.DMA,
                pltpu.SemaphoreType.DMA,
            ],
        ),
        out_shape=jax.ShapeDtypeStruct((n_rows, D_MODEL), F32),
        compiler_params=pltpu.CompilerParams(
            dimension_semantics=("arbitrary",),
            vmem_limit_bytes=VMEM_LIMIT),
    )(tab, pos, hn)


def _expert_mlp_body(tab_ref, xs_ref, wg_ref, wu_ref, wd_ref, ys_ref, wg_scr, wu_scr, wd_scr):
    b = pl.program_id(0)
    n_blocks = tab_ref[TAB_N_BLOCKS, 0]
    expert = tab_ref[TAB_BLK_EXPERT, b]
    prev_expert = tab_ref[TAB_BLK_EXPERT, jnp.maximum(b - 1, 0)]
    in_use = b < n_blocks

    @pl.when(in_use & ((b == 0) | (expert != prev_expert)))
    def _():
        wg_scr[...] = wg_ref[0].astype(BF16)
        wu_scr[...] = wu_ref[0].astype(BF16)
        wd_scr[...] = wd_ref[0].astype(BF16)

    @pl.when(in_use)
    def _():
        x = xs_ref[...].astype(BF16)
        gate = jnp.dot(x, wg_scr[...], preferred_element_type=F32)
        up = jnp.dot(x, wu_scr[...], preferred_element_type=F32)
        hidden = (jax.nn.silu(gate) * up).astype(BF16)
        ys_ref[...] = jnp.dot(hidden, wd_scr[...], preferred_element_type=F32)

    @pl.when(jnp.logical_not(in_use))
    def _():
        ys_ref[...] = jnp.zeros_like(ys_ref)


def _expert_mlp(tab, xs, w_gate, w_up, w_down):
    n_rows = xs.shape[0]
    bm = EXP_ROWS

    def block_of(b, tab):
        return jnp.minimum(b, tab[TAB_N_BLOCKS, 0] - 1)

    def expert_of(b, tab):
        return tab[TAB_BLK_EXPERT, block_of(b, tab)]

    return pl.pallas_call(
        _expert_mlp_body,
        name="expert_mlp",
        grid_spec=pltpu.PrefetchScalarGridSpec(
            num_scalar_prefetch=1,
            grid=(n_rows // bm,),
            in_specs=[
                pl.BlockSpec((bm, D_MODEL), lambda b, tab: (block_of(b, tab), 0)),
                pl.BlockSpec((1, D_MODEL, D_EXPERT), lambda b, tab: (expert_of(b, tab), 0, 0)),
                pl.BlockSpec((1, D_MODEL, D_EXPERT), lambda b, tab: (expert_of(b, tab), 0, 0)),
                pl.BlockSpec((1, D_EXPERT, D_MODEL), lambda b, tab: (expert_of(b, tab), 0, 0)),
            ],
            out_specs=pl.BlockSpec((bm, D_MODEL), lambda b, tab: (b, 0)),
            scratch_shapes=[
                pltpu.VMEM((D_MODEL, D_EXPERT), BF16),
                pltpu.VMEM((D_MODEL, D_EXPERT), BF16),
                pltpu.VMEM((D_EXPERT, D_MODEL), BF16),
            ],
        ),
        out_shape=jax.ShapeDtypeStruct((n_rows, D_MODEL), F32),
        compiler_params=pltpu.CompilerParams(
            dimension_semantics=("arbitrary",),
            vmem_limit_bytes=VMEM_LIMIT),
    )(tab, xs, w_gate, w_up, w_down)


def _combine_norm_body(pos_ref, h_ref, gcol_ref, gfin_ref, ys_ref, out_ref, y_scr, sem):
    tm = h_ref.shape[0]

    def issue(r, carry):
        for k in range(2):
            pltpu.make_async_copy(
                ys_ref.at[pl.ds(pos_ref[k, r], 1)], y_scr.at[k, pl.ds(r, 1)], sem).start()
        return carry

    lax.fori_loop(0, tm, issue, 0, unroll=8)
    for k in range(2):
        pltpu.make_async_copy(ys_ref.at[pl.ds(0, tm)], y_scr.at[k], sem).wait()

    gates = gcol_ref[...]
    moe = y_scr[0] * gates[:, 0:1] + y_scr[1] * gates[:, 1:2]
    h = h_ref[...] + moe
    ms = jnp.mean(h * h, axis=-1, keepdims=True)
    out_ref[...] = h * lax.rsqrt(ms + EPS) * gfin_ref[...]


def _combine_norm(pos, h, gcol, gfin, ys):
    t = h.shape[0]
    tm = COMB_ROWS
    return pl.pallas_call(
        _combine_norm_body,
        name="combine_norm",
        grid=(t // tm,),
        in_specs=[
            pl.BlockSpec((2, tm), lambda i: (0, i), memory_space=pltpu.SMEM),
            pl.BlockSpec((tm, D_MODEL), lambda i: (i, 0)),
            pl.BlockSpec((tm, LANES), lambda i: (i, 0)),
            pl.BlockSpec((1, D_MODEL), lambda i: (0, 0)),
            pl.BlockSpec(memory_space=pl.ANY),
        ],
        out_specs=pl.BlockSpec((tm, D_MODEL), lambda i: (i, 0)),
        out_shape=jax.ShapeDtypeStruct((t, D_MODEL), F32),
        scratch_shapes=[
            pltpu.VMEM((2, tm, D_MODEL), F32),
            pltpu.SemaphoreType.DMA,
        ],
        compiler_params=pltpu.CompilerParams(
            dimension_semantics=("arbitrary",),
            vmem_limit_bytes=VMEM_LIMIT),
    )(pos, h, gcol, gfin, ys)


def kernel(x, norm_mix_g, w_in, gmlp_v_norm_g, gmlp_ws, gmlp_bs, conv_w, out_norm_gmlp_g,
           out_norm_conv_g, w_out, norm_ffn_g, router_group_w, router_group_b, router_expert_w,
           router_expert_b, expert_w_gate, expert_w_up, expert_w_down, norm_final_g):
    batch, seq_len, d_model = x.shape
    t = batch * seq_len
    assert w_in.shape[0] == 1, "single-layer block"
    n_col_groups = GMLP_WIDTH // MIX_COLS
    n_rows = t * 2 + N_EXPERTS * EXP_ROWS
    x2 = x.reshape(t, d_model)

    vng = gmlp_v_norm_g[0].reshape(n_col_groups, 1, MIX_COLS)
    bsb = jnp.broadcast_to(gmlp_bs[0][:, :, None], (GMLP_HEADS, GMLP_BLOCK, HEAD_DIM))
    taps = conv_w[0].reshape(CONV_K, n_col_groups, MIX_COLS).transpose(1, 0, 2)
    unused = LANES - N_EXPERTS - N_GROUPS
    wr = jnp.concatenate(
        [router_expert_w[0], router_group_w[0], jnp.zeros((d_model, unused), F32)],
        axis=1).astype(BF16)
    br = jnp.concatenate(
        [router_expert_b[0], router_group_b[0], jnp.zeros((unused,), F32)])[None, :]

    ya, yb, sa, sb = _mixer_proj(x2, norm_mix_g[0][None, :], w_in[0].astype(BF16), vng,
                                 gmlp_ws[0], bsb, taps, seq_len)
    h, hn, meta, gcol, cnt = _out_route(
        ya, yb, sa, sb, x2, out_norm_gmlp_g[0][None, :], out_norm_conv_g[0][None, :],
        w_out[0].astype(BF16), norm_ffn_g[0][None, :], wr, br)
    pos, tab = _route_table(meta, cnt)
    xs = _permute_rows(tab, pos, hn, n_rows)
    ys = _expert_mlp(tab, xs, expert_w_gate[0], expert_w_up[0], expert_w_down[0])
    out = _combine_norm(pos, h, gcol, norm_final_g[None, :], ys)
    return out.reshape(batch, seq_len, d_model)
```

```python
import functools

import jax
import jax.numpy as jnp
from jax import lax
from jax.experimental import pallas as pl
from jax.experimental.pallas import tpu as pltpu

F32 = jnp.float32
BF16 = jnp.bfloat16
I32 = jnp.int32

D_MODEL = 2048
CHUNK = 64
GMLP_WIDTH = 1024
GMLP_HEADS = 8
HEAD_DIM = 128
GMLP_BLOCK = 128
CONV_CH = 1024
CONV_K = 3
N_GROUPS = 4
EXPERTS_PER_GROUP = 8
N_EXPERTS = 32
D_EXPERT = 512
EPS = 1e-6

LANES = 128
SUBLANES = 8

MIX_ROWS = 1024
MIX_COLS = 256
OUT_ROWS = 512
OUT_CHUNK = 256
EXP_ROWS = 256
FIN_ROWS = 512
ZERO_ROWS = 128
GROUP_LANE0 = N_EXPERTS
MASKED = -1e30
VMEM_LIMIT = 56 * 1024 * 1024

XS_COLS = D_MODEL + LANES
META_TOKEN = 0
META_GATE0 = 1
META_POS0 = 2
META_GATE1 = 3
META_POS1 = 4

TAB_ORDER = 0
TAB_EXPERT = 1
TAB_N_PAGES = 2
TAB_ROWS = 8
TAB_LANES = 256


def _rms_scale(sumsq, width):
    return lax.rsqrt(sumsq / width + EPS)


def _mixer_proj_body(seq_len, x_ref, gmix_ref, wu_ref, wv_ref, wbg_ref, wcg_ref, whv_ref,
                     vng_ref, ws_ref, bsb_ref, cw_ref,
                     ya_ref, yb_ref, sa_ref, sb_ref, xn_scr, carry_scr):
    i = pl.program_id(0)
    c = pl.program_id(1)
    tm = x_ref.shape[0]
    cw = ya_ref.shape[1]
    heads_per_step = cw // HEAD_DIM

    @pl.when(c == 0)
    def _():
        x = x_ref[...]
        ms = jnp.mean(x * x, axis=-1, keepdims=True)
        xn_scr[...] = (x * lax.rsqrt(ms + EPS) * gmix_ref[...]).astype(BF16)
        sa_ref[...] = jnp.zeros_like(sa_ref)
        sb_ref[...] = jnp.zeros_like(sb_ref)

    xn = xn_scr[...]
    project = lambda w_ref: jnp.dot(xn, w_ref[...], preferred_element_type=F32)
    u = jax.nn.gelu(project(wu_ref))
    v = jax.nn.gelu(project(wv_ref))
    bg = project(wbg_ref)
    cg = project(wcg_ref)
    hv = project(whv_ref)

    pos_i = lax.broadcasted_iota(I32, (GMLP_BLOCK, GMLP_BLOCK), 0)
    pos_j = lax.broadcasted_iota(I32, (GMLP_BLOCK, GMLP_BLOCK), 1)
    chunk_shift = CHUNK.bit_length() - 1
    causal = (pos_i >> chunk_shift) >= (pos_j >> chunk_shift)
    vng = vng_ref[0]
    ya_heads = []
    for j in range(heads_per_step):
        head = c * heads_per_step + j
        lanes = slice(j * HEAD_DIM, (j + 1) * HEAD_DIM)
        vj = v[:, lanes]
        ms = jnp.mean(vj * vj, axis=-1, keepdims=True)
        vn = (vj * lax.rsqrt(ms + EPS) * vng[:, lanes]).astype(BF16)
        w_mix = jnp.where(causal, ws_ref[head], 0.0).astype(BF16)
        bias = bsb_ref[head]
        mixed = [
            jnp.dot(w_mix, vn[p * GMLP_BLOCK:(p + 1) * GMLP_BLOCK, :],
                    preferred_element_type=F32) + bias
            for p in range(tm // GMLP_BLOCK)
        ]
        ya_heads.append(u[:, lanes] * jnp.concatenate(mixed, axis=0))
    ya = jnp.concatenate(ya_heads, axis=1) if heads_per_step > 1 else ya_heads[0]

    z = cg * hv
    prev = carry_scr[c]
    seq_start = (i * tm) % seq_len == 0
    prev = jnp.where(seq_start, 0.0, prev)
    row = lax.broadcasted_iota(I32, (tm, cw), 0)
    z1 = jnp.where(row == 0, prev[SUBLANES - 1:SUBLANES, :], pltpu.roll(z, 1, axis=0))
    z2 = jnp.where(row == 0, prev[SUBLANES - 2:SUBLANES - 1, :],
                   jnp.where(row == 1, prev[SUBLANES - 1:SUBLANES, :],
                             pltpu.roll(z, 2, axis=0)))
    taps = cw_ref[0]
    conv = taps[0:1, :] * z2 + taps[1:2, :] * z1 + taps[2:3, :] * z
    yb = bg * conv
    carry_scr[c] = z[tm - SUBLANES:tm, :]

    ya_ref[...] = ya.astype(BF16)
    yb_ref[...] = yb.astype(BF16)
    ya2 = ya * ya
    yb2 = yb * yb
    sa_ref[...] += sum(ya2[:, k * LANES:(k + 1) * LANES] for k in range(cw // LANES))
    sb_ref[...] += sum(yb2[:, k * LANES:(k + 1) * LANES] for k in range(cw // LANES))


def _mixer_proj(x2, gmix, w_in, vng, ws, bsb, conv_taps, seq_len):
    t = x2.shape[0]
    cw = MIX_COLS
    tm = MIX_ROWS
    n_groups = GMLP_WIDTH // cw
    grid = (t // tm, n_groups)
    w_part = lambda k: pl.BlockSpec((D_MODEL, cw), lambda i, c: (0, k * n_groups + c))
    return pl.pallas_call(
        functools.partial(_mixer_proj_body, seq_len),
        name="mixer_proj",
        grid=grid,
        in_specs=[
            pl.BlockSpec((tm, D_MODEL), lambda i, c: (i, 0)),
            pl.BlockSpec((1, D_MODEL), lambda i, c: (0, 0)),
            w_part(0), w_part(1), w_part(2), w_part(3), w_part(4),
            pl.BlockSpec((1, 1, cw), lambda i, c: (c, 0, 0)),
            pl.BlockSpec((GMLP_HEADS, GMLP_BLOCK, GMLP_BLOCK), lambda i, c: (0, 0, 0)),
            pl.BlockSpec((GMLP_HEADS, GMLP_BLOCK, HEAD_DIM), lambda i, c: (0, 0, 0)),
            pl.BlockSpec((1, CONV_K, cw), lambda i, c: (c, 0, 0)),
        ],
        out_specs=[
            pl.BlockSpec((tm, cw), lambda i, c: (i, c)),
            pl.BlockSpec((tm, cw), lambda i, c: (i, c)),
            pl.BlockSpec((tm, LANES), lambda i, c: (i, 0)),
            pl.BlockSpec((tm, LANES), lambda i, c: (i, 0)),
        ],
        out_shape=[
            jax.ShapeDtypeStruct((t, GMLP_WIDTH), BF16),
            jax.ShapeDtypeStruct((t, CONV_CH), BF16),
            jax.ShapeDtypeStruct((t, LANES), F32),
            jax.ShapeDtypeStruct((t, LANES), F32),
        ],
        scratch_shapes=[
            pltpu.VMEM((tm, D_MODEL), BF16),
            pltpu.VMEM((n_groups, SUBLANES, cw), F32),
        ],
        compiler_params=pltpu.CompilerParams(
            dimension_semantics=("arbitrary", "arbitrary"),
            vmem_limit_bytes=VMEM_LIMIT),
    )(x2, gmix, w_in, w_in, w_in, w_in, w_in, vng, ws, bsb, conv_taps)


def _lane_cumsum(v, lane):
    shift = 1
    while shift < N_EXPERTS:
        v = v + jnp.where(lane >= shift, pltpu.roll(v, shift, axis=1), 0.0)
        shift *= 2
    return v


def _lane_dense(col_values, lane):
    tile = jnp.zeros(lane.shape, F32)
    for k, col in enumerate(col_values):
        tile = jnp.where(lane == k, col, tile)
    return tile.T[0:SUBLANES, :]


def _finish_pages(n_page_rows, cnt_ref, ptab_ref, xs_ref, tab_ref, zero_scr, sem):
    page_shift = EXP_ROWS.bit_length() - 1
    n_slots = tab_ref.shape[1]
    zero_scr[...] = jnp.zeros_like(zero_scr)

    def clear(s, carry):
        for row in range(TAB_ROWS):
            tab_ref[row, s] = 0
        return carry

    lax.fori_loop(0, n_slots, clear, 0)

    def pages_of(e):
        return (cnt_ref[0, e] + (EXP_ROWS - 1)) >> page_shift

    def list_pages(e, first_slot):
        def put(j, carry):
            tab_ref[TAB_ORDER, first_slot + j] = ptab_ref[e, j]
            tab_ref[TAB_EXPERT, first_slot + j] = e
            return carry

        n_pages_e = pages_of(e)
        lax.fori_loop(0, n_pages_e, put, 0)
        return first_slot + n_pages_e

    n_pages = lax.fori_loop(0, N_EXPERTS, list_pages, 0)
    tab_ref[TAB_N_PAGES, 0] = n_pages
    last_page = tab_ref[TAB_ORDER, n_pages - 1]
    last_expert = tab_ref[TAB_EXPERT, n_pages - 1]

    def repeat_last(s, carry):
        tab_ref[TAB_ORDER, s] = last_page
        tab_ref[TAB_EXPERT, s] = last_expert
        return carry

    lax.fori_loop(n_pages, n_slots, repeat_last, 0)

    def pad_copies(e):
        used = cnt_ref[0, e] & (EXP_ROWS - 1)
        last = ptab_ref[e, jnp.maximum(pages_of(e) - 1, 0)]
        return _zero_range_copies(zero_scr, xs_ref, sem, last * EXP_ROWS + used,
                                  (EXP_ROWS - used) & (EXP_ROWS - 1))

    def tail_copies(p):
        return [
            pltpu.make_async_copy(
                zero_scr,
                xs_ref.at[pl.ds(pl.multiple_of(p * EXP_ROWS + part * ZERO_ROWS, ZERO_ROWS),
                                ZERO_ROWS)],
                sem)
            for part in range(EXP_ROWS // ZERO_ROWS)
        ]

    def start_pads(e, carry):
        for needed, copy in pad_copies(e):
            pl.when(needed)(copy.start)
        return carry

    def wait_pads(e, carry):
        for needed, copy in pad_copies(e):
            pl.when(needed)(copy.wait)
        return carry

    def start_tail(p, carry):
        for copy in tail_copies(p):
            copy.start()
        return carry

    def wait_tail(p, carry):
        for copy in tail_copies(p):
            copy.wait()
        return carry

    total_pages = n_page_rows // EXP_ROWS
    lax.fori_loop(0, N_EXPERTS, start_pads, 0)
    lax.fori_loop(n_pages, total_pages, start_tail, 0)
    lax.fori_loop(0, N_EXPERTS, wait_pads, 0)
    lax.fori_loop(n_pages, total_pages, wait_tail, 0)


def _out_route_body(trash_row0, ya_ref, yb_ref, sa_ref, sb_ref, x_ref, ga_ref, gb_ref, wout_ref,
                    gffn_ref, wr_ref, br_ref, h_ref, xs_ref, tab_ref,
                    row_buf, pos_vmem, pos_smem, st_scr, ptab_scr, cnt_vmem, ptab_vmem,
                    cnt_smem, ptab_smem, zero_scr, row_sem, pos_sem, fill_sem):
    i = pl.program_id(0)
    n_steps = pl.num_programs(0) - 1
    tm = x_ref.shape[0]
    slot = lax.rem(i, 2)
    prev = 1 - slot

    def row_copy(r, k):
        return pltpu.make_async_copy(
            row_buf.at[prev, pl.ds(r, 1)], xs_ref.at[pl.ds(pos_smem[prev, k, r], 1)], row_sem)

    def wait_step_rows():
        for _ in range(2):
            pltpu.make_async_copy(row_buf.at[0], xs_ref.at[pl.ds(0, tm)], row_sem).wait()

    def pos_copy(s):
        return pltpu.make_async_copy(pos_vmem, pos_smem.at[s], pos_sem)

    @pl.when(i == 0)
    def _():
        st_scr[...] = jnp.zeros_like(st_scr)
        ptab_scr[...] = jnp.zeros_like(ptab_scr)
        row_buf[1] = jnp.zeros((tm, XS_COLS), F32)

        def fill(r, carry):
            pos_smem[1, 0, r] = trash_row0 + r
            pos_smem[1, 1, r] = trash_row0 + tm + r
            return carry

        lax.fori_loop(0, tm, fill, 0)

    @pl.when(i > 0)
    def _():
        wait_step_rows()
        pos_copy(prev).wait()

    @pl.when(i < n_steps)
    def _():
        rows_per_chunk = tm // (D_MODEL // OUT_CHUNK)

        ra = _rms_scale(jnp.sum(sa_ref[...], axis=-1, keepdims=True), GMLP_WIDTH)
        rb = _rms_scale(jnp.sum(sb_ref[...], axis=-1, keepdims=True), CONV_CH)
        yna = (ya_ref[...].astype(F32) * ra * ga_ref[...]).astype(BF16)
        ynb = (yb_ref[...].astype(F32) * rb * gb_ref[...]).astype(BF16)
        yn = jnp.concatenate([yna, ynb], axis=1)
        for j in range(D_MODEL // OUT_CHUNK):
            cols = slice(j * OUT_CHUNK, (j + 1) * OUT_CHUNK)
            h_ref[:, cols] = x_ref[:, cols] + jnp.dot(
                yn, wout_ref[:, cols], preferred_element_type=F32)
            for r in range(j * rows_per_chunk, (j + 1) * rows_per_chunk):
                for k in range(2):
                    row_copy(r, k).start()
        h = h_ref[...]

        ms = jnp.mean(h * h, axis=-1, keepdims=True)
        hn = h * lax.rsqrt(ms + EPS) * gffn_ref[...]
        logits = jnp.dot(hn.astype(BF16), wr_ref[...], preferred_element_type=F32) + br_ref[...]

        lane = lax.broadcasted_iota(I32, (tm, LANES), 1)
        lane_f = lane.astype(F32)
        big = float(LANES)

        is_g = (lane >= GROUP_LANE0) & (lane < GROUP_LANE0 + N_GROUPS)
        lg = jnp.where(is_g, logits, MASKED)
        eg = jnp.where(is_g, jnp.exp(lg - jnp.max(lg, axis=-1, keepdims=True)), 0.0)
        pg = eg / jnp.sum(eg, axis=-1, keepdims=True)
        pg_top = jnp.max(pg, axis=-1, keepdims=True)
        g_idx = jnp.min(jnp.where(is_g & (pg == pg_top), lane_f - GROUP_LANE0, big),
                        axis=-1, keepdims=True).astype(I32)

        group_shift = EXPERTS_PER_GROUP.bit_length() - 1
        in_grp = (lane < N_EXPERTS) & ((lane >> group_shift) == g_idx)
        le = jnp.where(in_grp, logits, MASKED)
        ee = jnp.where(in_grp, jnp.exp(le - jnp.max(le, axis=-1, keepdims=True)), 0.0)
        q = ee / jnp.sum(ee, axis=-1, keepdims=True)
        q1 = jnp.max(jnp.where(in_grp, q, -1.0), axis=-1, keepdims=True)
        e1 = jnp.min(jnp.where(in_grp & (q == q1), lane_f, big), axis=-1, keepdims=True)
        rest = in_grp & (lane_f != e1)
        q2 = jnp.max(jnp.where(rest, q, -1.0), axis=-1, keepdims=True)
        e2 = jnp.min(jnp.where(rest & (q == q2), lane_f, big), axis=-1, keepdims=True)
        qs = q1 + q2
        gate1 = pg_top * (q1 / qs)
        gate2 = pg_top * (q2 / qs)

        is1 = lane_f == e1
        is2 = lane_f == e2
        sel = jnp.where(is1 | is2, 1.0, 0.0)
        t_i = lax.broadcasted_iota(I32, (tm, tm), 0)
        t_j = lax.broadcasted_iota(I32, (tm, tm), 1)
        before = jnp.where(t_j < t_i, 1.0, 0.0).astype(BF16)
        run0 = st_scr[0:1, :]
        cur_page = st_scr[1:2, :]
        next_free = st_scr[2:3, :]
        cum = jnp.dot(before, sel.astype(BF16), preferred_element_type=F32) + run0

        lane_row = lax.broadcasted_iota(I32, (1, LANES), 1)
        run1 = run0 + jnp.sum(sel, axis=0, keepdims=True)
        pages0 = jnp.ceil(run0 / EXP_ROWS)
        n_new = jnp.ceil(run1 / EXP_ROWS) - pages0
        first_new = next_free + _lane_cumsum(n_new, lane_row) - n_new

        def pick(mask, per_expert):
            return jnp.sum(jnp.where(mask, per_expert, 0.0), axis=-1, keepdims=True)

        def place(mask):
            rank = pick(mask, cum)
            page_idx = jnp.floor(rank / EXP_ROWS)
            owned = pick(mask, pages0)
            page = jnp.where(page_idx < owned, pick(mask, cur_page),
                             pick(mask, first_new) + (page_idx - owned))
            return page * EXP_ROWS + (rank - page_idx * EXP_ROWS)

        pos1 = place(is1)
        pos2 = place(is2)

        st_scr[0:1, :] = run1
        st_scr[1:2, :] = jnp.where(n_new > 0, first_new + n_new - 1, cur_page)
        st_scr[2:3, :] = next_free + jnp.sum(n_new, axis=-1, keepdims=True)

        per_expert = jnp.concatenate(
            [pages0, n_new, first_new, jnp.zeros((LANES - 3, LANES), F32)], axis=0).T
        owned_c = per_expert[0:N_EXPERTS, 0:1]
        n_new_c = per_expert[0:N_EXPERTS, 1:2]
        first_c = per_expert[0:N_EXPERTS, 2:3]
        idx = lax.broadcasted_iota(I32, (N_EXPERTS, LANES), 1).astype(F32)
        fresh = (idx >= owned_c) & (idx < owned_c + n_new_c)
        ptab_scr[...] = jnp.where(fresh, first_c + (idx - owned_c), ptab_scr[...])
        cnt_vmem[...] = jnp.broadcast_to(run1, cnt_vmem.shape).astype(I32)
        ptab_vmem[...] = ptab_scr[...].astype(I32)

        token = (i * tm + lax.broadcasted_iota(I32, (tm, 1), 0) + 1).astype(F32)
        meta = jnp.zeros((tm, LANES), F32)
        for k, col in ((META_TOKEN, token), (META_GATE0, gate1), (META_POS0, pos1),
                       (META_GATE1, gate2), (META_POS1, pos2)):
            meta = jnp.where(lane == k, col, meta)
        row_buf[slot, :, 0:D_MODEL] = hn
        row_buf[slot, :, D_MODEL:XS_COLS] = meta
        pos_vmem[...] = _lane_dense([pos1, pos2], lane).astype(I32)
        pos_copy(slot).start()

    @pl.when(i == n_steps)
    def _():
        def issue(r, carry):
            for k in range(2):
                row_copy(r, k).start()
            return carry

        lax.fori_loop(0, tm, issue, 0, unroll=8)
        wait_step_rows()

        for vec, scal in ((cnt_vmem, cnt_smem), (ptab_vmem, ptab_smem)):
            copy = pltpu.make_async_copy(vec, scal, fill_sem)
            copy.start()
            copy.wait()
        _finish_pages(trash_row0, cnt_smem, ptab_smem, xs_ref, tab_ref, zero_scr, fill_sem)


def _out_route(ya, yb, sa, sb, x2, ga, gb, w_out, gffn, wr, br, n_page_rows):
    t = x2.shape[0]
    tm = OUT_ROWS
    n_steps = t // tm
    const = lambda i: (0, 0)
    rows = lambda i: (jnp.minimum(i, n_steps - 1), 0)
    return pl.pallas_call(
        functools.partial(_out_route_body, n_page_rows),
        name="out_route",
        grid=(n_steps + 1,),
        in_specs=[
            pl.BlockSpec((tm, GMLP_WIDTH), rows),
            pl.BlockSpec((tm, CONV_CH), rows),
            pl.BlockSpec((tm, LANES), rows),
            pl.BlockSpec((tm, LANES), rows),
            pl.BlockSpec((tm, D_MODEL), rows),
            pl.BlockSpec((1, GMLP_WIDTH), const),
            pl.BlockSpec((1, CONV_CH), const),
            pl.BlockSpec((D_MODEL, D_MODEL), const, pipeline_mode=pl.Buffered(1)),
            pl.BlockSpec((1, D_MODEL), const),
            pl.BlockSpec((D_MODEL, LANES), const),
            pl.BlockSpec((1, LANES), const),
        ],
        out_specs=[
            pl.BlockSpec((tm, D_MODEL), rows),
            pl.BlockSpec(memory_space=pl.ANY),
            pl.BlockSpec(memory_space=pltpu.SMEM),
        ],
        out_shape=[
            jax.ShapeDtypeStruct((t, D_MODEL), F32),
            jax.ShapeDtypeStruct((n_page_rows + 2 * tm, XS_COLS), F32),
            jax.ShapeDtypeStruct((TAB_ROWS, TAB_LANES), I32),
        ],
        scratch_shapes=[
            pltpu.VMEM((2, tm, XS_COLS), F32),
            pltpu.VMEM((SUBLANES, tm), I32),
            pltpu.SMEM((2, SUBLANES, tm), I32),
            pltpu.VMEM((SUBLANES, LANES), F32),
            pltpu.VMEM((N_EXPERTS, LANES), F32),
            pltpu.VMEM((SUBLANES, LANES), I32),
            pltpu.VMEM((N_EXPERTS, LANES), I32),
            pltpu.SMEM((SUBLANES, LANES), I32),
            pltpu.SMEM((N_EXPERTS, LANES), I32),
            pltpu.VMEM((ZERO_ROWS, XS_COLS), F32),
            pltpu.SemaphoreType.DMA,
            pltpu.SemaphoreType.DMA,
            pltpu.SemaphoreType.DMA,
        ],
        compiler_params=pltpu.CompilerParams(
            dimension_semantics=("arbitrary",),
            vmem_limit_bytes=VMEM_LIMIT),
    )(ya, yb, sa, sb, x2, ga, gb, w_out, gffn, wr, br)


def _zero_range_copies(zero_scr, xs_ref, sem, start, length):
    head = length & (SUBLANES - 1)
    copies = []
    for j in range(SUBLANES - 1):
        copy = pltpu.make_async_copy(
            zero_scr.at[pl.ds(0, 1)], xs_ref.at[pl.ds(start + j, 1)], sem)
        copies.append((j < head, copy))
    body_start = start + head
    body = length - head
    size = ZERO_ROWS
    while size >= SUBLANES:
        offset = pl.multiple_of(body_start + (body & ~(2 * size - 1)), SUBLANES)
        copy = pltpu.make_async_copy(
            zero_scr.at[pl.ds(0, size)], xs_ref.at[pl.ds(offset, size)], sem)
        copies.append(((body & size) != 0, copy))
        size //= 2
    return copies


def _expert_mlp_body(n_tokens, tab_ref, xs_ref, wg_ref, wu_ref, wd_ref, out2_ref,
                     wg_scr, wu_scr, wd_scr, out_buf, dest_vmem, dest_smem, row_sem, dest_sem):
    s = pl.program_id(0)
    n_pages = tab_ref[TAB_N_PAGES, 0]
    bm = xs_ref.shape[0]
    cur = lax.rem(s, 3)
    prv = lax.rem(s + 2, 3)
    par = lax.rem(s, 2)
    ppar = 1 - par
    trash_row0 = 2 * n_tokens
    expert = tab_ref[TAB_EXPERT, s]
    prev_expert = tab_ref[TAB_EXPERT, jnp.maximum(s - 1, 0)]

    def row_copy(r):
        return pltpu.make_async_copy(
            out_buf.at[prv, pl.ds(r, 1)], out2_ref.at[pl.ds(dest_smem[ppar, 0, r], 1)],
            row_sem.at[prv])

    def wait_rows(slot):
        pltpu.make_async_copy(out_buf.at[0], out2_ref.at[pl.ds(0, bm)], row_sem.at[slot]).wait()

    def dest_copy(p):
        return pltpu.make_async_copy(dest_vmem, dest_smem.at[p], dest_sem)

    @pl.when(s == 0)
    def _():
        out_buf[2] = jnp.zeros((bm, D_MODEL), F32)
        for half in range(2):
            clear = pltpu.make_async_copy(
                out_buf.at[2], out2_ref.at[pl.ds(trash_row0 + half * bm, bm)], dest_sem)
            clear.start()
            clear.wait()

        def fill(r, carry):
            dest_smem[1, 0, r] = trash_row0 + bm + r
            return carry

        lax.fori_loop(0, bm, fill, 0)

    @pl.when((s >= 2) & (s <= n_pages))
    def _():
        wait_rows(cur)

    @pl.when((s >= 1) & (s <= n_pages))
    def _():
        dest_copy(ppar).wait()

    @pl.when((s < n_pages) & ((s == 0) | (expert != prev_expert)))
    def _():
        wg_scr[...] = wg_ref[0].astype(BF16)
        wu_scr[...] = wu_ref[0].astype(BF16)
        wd_scr[...] = wd_ref[0].astype(BF16)

    @pl.when(s < n_pages)
    def _():
        n_groups = 8
        group_rows = bm // n_groups
        half_e = D_EXPERT // 2
        quarter_d = D_MODEL // 4

        def scatter_group(g):
            for r in range(g * group_rows, (g + 1) * group_rows):
                row_copy(r).start()

        x = xs_ref[:, 0:D_MODEL].astype(BF16)
        meta = xs_ref[:, D_MODEL:XS_COLS]
        gate_parts, up_parts = [], []
        for part in range(2):
            cols = slice(part * half_e, (part + 1) * half_e)
            gate_parts.append(jnp.dot(x, wg_scr[:, cols], preferred_element_type=F32))
            scatter_group(2 * part)
            up_parts.append(jnp.dot(x, wu_scr[:, cols], preferred_element_type=F32))
            scatter_group(2 * part + 1)
        gate = jnp.concatenate(gate_parts, axis=1)
        up = jnp.concatenate(up_parts, axis=1)
        hidden = (jax.nn.silu(gate) * up).astype(BF16)

        own_row = (tab_ref[TAB_ORDER, s] * bm
                   + lax.broadcasted_iota(I32, (bm, 1), 0)).astype(F32)
        local = lax.broadcasted_iota(I32, (bm, 1), 0).astype(F32)
        token = meta[:, META_TOKEN:META_TOKEN + 1]
        first = meta[:, META_POS0:META_POS0 + 1] == own_row
        weight = jnp.where(first, meta[:, META_GATE0:META_GATE0 + 1],
                           meta[:, META_GATE1:META_GATE1 + 1])
        plane = jnp.where(first, 0.0, float(n_tokens))
        dest = jnp.where(token > 0.0, plane + token - 1.0,
                         (trash_row0 + par * bm).astype(F32) + local)

        for part in range(4):
            cols = slice(part * quarter_d, (part + 1) * quarter_d)
            out_buf[cur, :, cols] = jnp.dot(
                hidden, wd_scr[:, cols], preferred_element_type=F32) * weight
            scatter_group(4 + part)

        lane = lax.broadcasted_iota(I32, (bm, LANES), 1)
        dest_vmem[...] = _lane_dense([dest], lane).astype(I32)
        dest_copy(par).start()

    @pl.when(s == n_pages)
    def _():
        def issue(r, carry):
            row_copy(r).start()
            return carry

        lax.fori_loop(0, bm, issue, 0, unroll=8)
        wait_rows(lax.rem(s + 1, 3))
        wait_rows(prv)


def _expert_mlp(tab, xs, w_gate, w_up, w_down, n_tokens, n_pages_max):
    bm = EXP_ROWS
    page_of = lambda s, tab: (tab[TAB_ORDER, s], 0)
    expert_of = lambda s, tab: (tab[TAB_EXPERT, s], 0, 0)
    return pl.pallas_call(
        functools.partial(_expert_mlp_body, n_tokens),
        name="expert_mlp",
        grid_spec=pltpu.PrefetchScalarGridSpec(
            num_scalar_prefetch=1,
            grid=(n_pages_max + 1,),
            in_specs=[
                pl.BlockSpec((bm, XS_COLS), page_of),
                pl.BlockSpec((1, D_MODEL, D_EXPERT), expert_of),
                pl.BlockSpec((1, D_MODEL, D_EXPERT), expert_of),
                pl.BlockSpec((1, D_EXPERT, D_MODEL), expert_of),
            ],
            out_specs=pl.BlockSpec(memory_space=pl.ANY),
            scratch_shapes=[
                pltpu.VMEM((D_MODEL, D_EXPERT), BF16),
                pltpu.VMEM((D_MODEL, D_EXPERT), BF16),
                pltpu.VMEM((D_EXPERT, D_MODEL), BF16),
                pltpu.VMEM((3, bm, D_MODEL), F32),
                pltpu.VMEM((SUBLANES, bm), I32),
                pltpu.SMEM((2, SUBLANES, bm), I32),
                pltpu.SemaphoreType.DMA((3,)),
                pltpu.SemaphoreType.DMA,
            ],
        ),
        out_shape=jax.ShapeDtypeStruct((2 * n_tokens + 2 * bm, D_MODEL), F32),
        compiler_params=pltpu.CompilerParams(
            dimension_semantics=("arbitrary",),
            vmem_limit_bytes=VMEM_LIMIT),
    )(tab, xs, w_gate, w_up, w_down)


def _final_norm_body(h_ref, y0_ref, y1_ref, gfin_ref, out_ref):
    h = h_ref[...] + (y0_ref[...] + y1_ref[...])
    ms = jnp.mean(h * h, axis=-1, keepdims=True)
    out_ref[...] = h * lax.rsqrt(ms + EPS) * gfin_ref[...]


def _final_norm(h, out2, gfin):
    t = h.shape[0]
    tm = FIN_ROWS
    plane1 = t // tm
    return pl.pallas_call(
        _final_norm_body,
        name="final_norm",
        grid=(t // tm,),
        in_specs=[
            pl.BlockSpec((tm, D_MODEL), lambda i: (i, 0)),
            pl.BlockSpec((tm, D_MODEL), lambda i: (i, 0)),
            pl.BlockSpec((tm, D_MODEL), lambda i: (i + plane1, 0)),
            pl.BlockSpec((1, D_MODEL), lambda i: (0, 0)),
        ],
        out_specs=pl.BlockSpec((tm, D_MODEL), lambda i: (i, 0)),
        out_shape=jax.ShapeDtypeStruct((t, D_MODEL), F32),
        compiler_params=pltpu.CompilerParams(
            dimension_semantics=("arbitrary",),
            vmem_limit_bytes=VMEM_LIMIT),
    )(h, out2, out2, gfin)


def kernel(x, norm_mix_g, w_in, gmlp_v_norm_g, gmlp_ws, gmlp_bs, conv_w, out_norm_gmlp_g,
           out_norm_conv_g, w_out, norm_ffn_g, router_group_w, router_group_b, router_expert_w,
           router_expert_b, expert_w_gate, expert_w_up, expert_w_down, norm_final_g):
    batch, seq_len, d_model = x.shape
    t = batch * seq_len
    assert w_in.shape[0] == 1, "single-layer block"
    assert t // EXP_ROWS <= LANES, "page table holds at most LANES pages per expert"
    n_col_groups = GMLP_WIDTH // MIX_COLS
    n_pages_max = 2 * t // EXP_ROWS + N_EXPERTS
    n_page_rows = n_pages_max * EXP_ROWS
    x2 = x.reshape(t, d_model)

    vng = gmlp_v_norm_g[0].reshape(n_col_groups, 1, MIX_COLS)
    bsb = jnp.broadcast_to(gmlp_bs[0][:, :, None], (GMLP_HEADS, GMLP_BLOCK, HEAD_DIM))
    taps = conv_w[0].reshape(CONV_K, n_col_groups, MIX_COLS).transpose(1, 0, 2)
    unused = LANES - N_EXPERTS - N_GROUPS
    wr = jnp.concatenate(
        [router_expert_w[0], router_group_w[0], jnp.zeros((d_model, unused), F32)],
        axis=1).astype(BF16)
    br = jnp.concatenate(
        [router_expert_b[0], router_group_b[0], jnp.zeros((unused,), F32)])[None, :]

    ya, yb, sa, sb = _mixer_proj(x2, norm_mix_g[0][None, :], w_in[0].astype(BF16), vng,
                                 gmlp_ws[0], bsb, taps, seq_len)
    h, xs, tab = _out_route(
        ya, yb, sa, sb, x2, out_norm_gmlp_g[0][None, :], out_norm_conv_g[0][None, :],
        w_out[0].astype(BF16), norm_ffn_g[0][None, :], wr, br, n_page_rows)
    out2 = _expert_mlp(tab, xs, expert_w_gate[0], expert_w_up[0], expert_w_down[0],
                       t, n_pages_max)
    out = _final_norm(h, out2, norm_final_g[None, :])
    return out.reshape(batch, seq_len, d_model)
```

```python
import functools

import jax
import jax.numpy as jnp
from jax import lax
from jax.experimental import pallas as pl
from jax.experimental.pallas import tpu as pltpu

F32 = jnp.float32
BF16 = jnp.bfloat16
I32 = jnp.int32

D_MODEL = 2048
CHUNK = 64
GMLP_WIDTH = 1024
GMLP_HEADS = 8
HEAD_DIM = 128
GMLP_BLOCK = 128
CONV_CH = 1024
CONV_K = 3
N_GROUPS = 4
EXPERTS_PER_GROUP = 8
N_EXPERTS = 32
D_EXPERT = 512
EPS = 1e-6

LANES = 128
SUBLANES = 8

MIX_ROWS = 1024
MIX_COLS = 256
OUT_ROWS = 512
OUT_CHUNK = 256
EXP_ROWS = 256
FIN_ROWS = 512
ZERO_ROWS = 128
GROUP_LANE0 = N_EXPERTS
MASKED = -1e30
VMEM_LIMIT = 56 * 1024 * 1024

XS_COLS = D_MODEL + LANES
ROW_TILES = D_MODEL // LANES
META_TOKEN = 0
META_GATE0 = 1
META_POS0 = 2
META_GATE1 = 3
META_POS1 = 4

TAB_ORDER = 0
TAB_EXPERT = 1
TAB_N_PAGES = 2
TAB_RUN_LEN = 3
TAB_WSLOT = 4
TAB_ROWS = 8
TAB_LANES = 256


def _rms_scale(sumsq, width):
    return lax.rsqrt(sumsq / width + EPS)


def _mixer_proj_body(seq_len, x_ref, gmix_ref, wu_ref, wv_ref, wbg_ref, wcg_ref, whv_ref,
                     vng_ref, ws_ref, bsb_ref, cw_ref,
                     ya_ref, yb_ref, sa_ref, sb_ref, xn_scr, carry_scr):
    i = pl.program_id(0)
    c = pl.program_id(1)
    tm = x_ref.shape[0]
    cw = ya_ref.shape[1]
    heads_per_step = cw // HEAD_DIM

    @pl.when(c == 0)
    def _():
        x = x_ref[...]
        ms = jnp.mean(x * x, axis=-1, keepdims=True)
        xn_scr[...] = (x * lax.rsqrt(ms + EPS) * gmix_ref[...]).astype(BF16)
        sa_ref[...] = jnp.zeros_like(sa_ref)
        sb_ref[...] = jnp.zeros_like(sb_ref)

    xn = xn_scr[...]
    project = lambda w_ref: jnp.dot(xn, w_ref[...], preferred_element_type=F32)
    u = jax.nn.gelu(project(wu_ref))
    v = jax.nn.gelu(project(wv_ref))
    bg = project(wbg_ref)
    cg = project(wcg_ref)
    hv = project(whv_ref)

    pos_i = lax.broadcasted_iota(I32, (GMLP_BLOCK, GMLP_BLOCK), 0)
    pos_j = lax.broadcasted_iota(I32, (GMLP_BLOCK, GMLP_BLOCK), 1)
    chunk_shift = CHUNK.bit_length() - 1
    causal = (pos_i >> chunk_shift) >= (pos_j >> chunk_shift)
    vng = vng_ref[0]
    ya_heads = []
    for j in range(heads_per_step):
        head = c * heads_per_step + j
        lanes = slice(j * HEAD_DIM, (j + 1) * HEAD_DIM)
        vj = v[:, lanes]
        ms = jnp.mean(vj * vj, axis=-1, keepdims=True)
        vn = (vj * lax.rsqrt(ms + EPS) * vng[:, lanes]).astype(BF16)
        w_mix = jnp.where(causal, ws_ref[head], 0.0).astype(BF16)
        bias = bsb_ref[head]
        mixed = [
            jnp.dot(w_mix, vn[p * GMLP_BLOCK:(p + 1) * GMLP_BLOCK, :],
                    preferred_element_type=F32) + bias
            for p in range(tm // GMLP_BLOCK)
        ]
        ya_heads.append(u[:, lanes] * jnp.concatenate(mixed, axis=0))
    ya = jnp.concatenate(ya_heads, axis=1) if heads_per_step > 1 else ya_heads[0]

    z = cg * hv
    prev = carry_scr[c]
    seq_start = (i * tm) % seq_len == 0
    prev = jnp.where(seq_start, 0.0, prev)
    row = lax.broadcasted_iota(I32, (tm, cw), 0)
    z1 = jnp.where(row == 0, prev[SUBLANES - 1:SUBLANES, :], pltpu.roll(z, 1, axis=0))
    z2 = jnp.where(row == 0, prev[SUBLANES - 2:SUBLANES - 1, :],
                   jnp.where(row == 1, prev[SUBLANES - 1:SUBLANES, :],
                             pltpu.roll(z, 2, axis=0)))
    taps = cw_ref[0]
    conv = taps[0:1, :] * z2 + taps[1:2, :] * z1 + taps[2:3, :] * z
    yb = bg * conv
    carry_scr[c] = z[tm - SUBLANES:tm, :]

    ya_ref[...] = ya.astype(BF16)
    yb_ref[...] = yb.astype(BF16)
    ya2 = ya * ya
    yb2 = yb * yb
    sa_ref[...] += sum(ya2[:, k * LANES:(k + 1) * LANES] for k in range(cw // LANES))
    sb_ref[...] += sum(yb2[:, k * LANES:(k + 1) * LANES] for k in range(cw // LANES))


def _mixer_proj(x2, gmix, w_in, vng, ws, bsb, conv_taps, seq_len):
    t = x2.shape[0]
    cw = MIX_COLS
    tm = MIX_ROWS
    n_groups = GMLP_WIDTH // cw
    grid = (t // tm, n_groups)
    w_part = lambda k: pl.BlockSpec((D_MODEL, cw), lambda i, c: (0, k * n_groups + c))
    return pl.pallas_call(
        functools.partial(_mixer_proj_body, seq_len),
        name="mixer_proj",
        grid=grid,
        in_specs=[
            pl.BlockSpec((tm, D_MODEL), lambda i, c: (i, 0)),
            pl.BlockSpec((1, D_MODEL), lambda i, c: (0, 0)),
            w_part(0), w_part(1), w_part(2), w_part(3), w_part(4),
            pl.BlockSpec((1, 1, cw), lambda i, c: (c, 0, 0)),
            pl.BlockSpec((GMLP_HEADS, GMLP_BLOCK, GMLP_BLOCK), lambda i, c: (0, 0, 0)),
            pl.BlockSpec((GMLP_HEADS, GMLP_BLOCK, HEAD_DIM), lambda i, c: (0, 0, 0)),
            pl.BlockSpec((1, CONV_K, cw), lambda i, c: (c, 0, 0)),
        ],
        out_specs=[
            pl.BlockSpec((tm, cw), lambda i, c: (i, c)),
            pl.BlockSpec((tm, cw), lambda i, c: (i, c)),
            pl.BlockSpec((tm, LANES), lambda i, c: (i, 0)),
            pl.BlockSpec((tm, LANES), lambda i, c: (i, 0)),
        ],
        out_shape=[
            jax.ShapeDtypeStruct((t, GMLP_WIDTH), BF16),
            jax.ShapeDtypeStruct((t, CONV_CH), BF16),
            jax.ShapeDtypeStruct((t, LANES), F32),
            jax.ShapeDtypeStruct((t, LANES), F32),
        ],
        scratch_shapes=[
            pltpu.VMEM((tm, D_MODEL), BF16),
            pltpu.VMEM((n_groups, SUBLANES, cw), F32),
        ],
        compiler_params=pltpu.CompilerParams(
            dimension_semantics=("arbitrary", "arbitrary"),
            vmem_limit_bytes=VMEM_LIMIT),
    )(x2, gmix, w_in, w_in, w_in, w_in, w_in, vng, ws, bsb, conv_taps)


def _lane_cumsum(v, lane):
    shift = 1
    while shift < N_EXPERTS:
        v = v + jnp.where(lane >= shift, pltpu.roll(v, shift, axis=1), 0.0)
        shift *= 2
    return v


def _lane_dense(col_values, lane):
    tile = jnp.zeros(lane.shape, F32)
    for k, col in enumerate(col_values):
        tile = jnp.where(lane == k, col, tile)
    return tile.T[0:SUBLANES, :]


def _finish_pages(n_page_rows, cnt_ref, ptab_ref, xs_ref, tab_ref, zero_scr, sem):
    page_shift = EXP_ROWS.bit_length() - 1
    n_slots = tab_ref.shape[1]
    zero_scr[...] = jnp.zeros_like(zero_scr)

    def clear(s, carry):
        for row in range(TAB_ROWS):
            tab_ref[row, s] = 0
        return carry

    lax.fori_loop(0, n_slots, clear, 0)

    def pages_of(e):
        return (cnt_ref[0, e] + (EXP_ROWS - 1)) >> page_shift

    def list_pages(e, carry):
        first_slot, run = carry

        def put(j, carry):
            tab_ref[TAB_ORDER, first_slot + j] = ptab_ref[e, j]
            tab_ref[TAB_EXPERT, first_slot + j] = e
            tab_ref[TAB_WSLOT, first_slot + j] = run & 1
            return carry

        n_pages_e = pages_of(e)
        lax.fori_loop(0, n_pages_e, put, 0)

        @pl.when(n_pages_e > 0)
        def _():
            tab_ref[TAB_RUN_LEN, first_slot] = n_pages_e

        return first_slot + n_pages_e, run + jnp.where(n_pages_e > 0, 1, 0)

    n_pages, _ = lax.fori_loop(0, N_EXPERTS, list_pages, (0, 0))
    tab_ref[TAB_N_PAGES, 0] = n_pages
    last_page = tab_ref[TAB_ORDER, n_pages - 1]
    last_expert = tab_ref[TAB_EXPERT, n_pages - 1]

    def repeat_last(s, carry):
        tab_ref[TAB_ORDER, s] = last_page
        tab_ref[TAB_EXPERT, s] = last_expert
        return carry

    lax.fori_loop(n_pages, n_slots, repeat_last, 0)

    def pad_copies(e):
        used = cnt_ref[0, e] & (EXP_ROWS - 1)
        last = ptab_ref[e, jnp.maximum(pages_of(e) - 1, 0)]
        return _zero_range_copies(zero_scr, xs_ref, sem, last * EXP_ROWS + used,
                                  (EXP_ROWS - used) & (EXP_ROWS - 1))

    def tail_copies(p):
        return [
            pltpu.make_async_copy(
                zero_scr,
                xs_ref.at[pl.ds(pl.multiple_of(p * EXP_ROWS + part * ZERO_ROWS, ZERO_ROWS),
                                ZERO_ROWS)],
                sem)
            for part in range(EXP_ROWS // ZERO_ROWS)
        ]

    def start_pads(e, carry):
        for needed, copy in pad_copies(e):
            pl.when(needed)(copy.start)
        return carry

    def wait_pads(e, carry):
        for needed, copy in pad_copies(e):
            pl.when(needed)(copy.wait)
        return carry

    def start_tail(p, carry):
        for copy in tail_copies(p):
            copy.start()
        return carry

    def wait_tail(p, carry):
        for copy in tail_copies(p):
            copy.wait()
        return carry

    total_pages = n_page_rows // EXP_ROWS
    lax.fori_loop(0, N_EXPERTS, start_pads, 0)
    lax.fori_loop(n_pages, total_pages, start_tail, 0)
    lax.fori_loop(0, N_EXPERTS, wait_pads, 0)
    lax.fori_loop(n_pages, total_pages, wait_tail, 0)


def _out_route_body(trash_row0, ya_ref, yb_ref, sa_ref, sb_ref, x_ref, ga_ref, gb_ref, wout_ref,
                    gffn_ref, wr_ref, br_ref, h_ref, xs_ref, tab_ref,
                    row_buf, pos_vmem, pos_smem, st_scr, ptab_scr, cnt_vmem, ptab_vmem,
                    cnt_smem, ptab_smem, zero_scr, row_sem, pos_sem, fill_sem):
    i = pl.program_id(0)
    n_steps = pl.num_programs(0) - 1
    tm = x_ref.shape[0]
    slot = lax.rem(i, 2)
    prev = 1 - slot

    def row_copy(r, k):
        return pltpu.make_async_copy(
            row_buf.at[prev, pl.ds(r, 1)], xs_ref.at[pl.ds(pos_smem[prev, k, r], 1)], row_sem)

    def wait_step_rows():
        for _ in range(2):
            pltpu.make_async_copy(row_buf.at[0], xs_ref.at[pl.ds(0, tm)], row_sem).wait()

    def pos_copy(s):
        return pltpu.make_async_copy(pos_vmem, pos_smem.at[s], pos_sem)

    @pl.when(i == 0)
    def _():
        st_scr[...] = jnp.zeros_like(st_scr)
        ptab_scr[...] = jnp.zeros_like(ptab_scr)
        row_buf[1] = jnp.zeros((tm, XS_COLS), F32)

        def fill(r, carry):
            pos_smem[1, 0, r] = trash_row0 + r
            pos_smem[1, 1, r] = trash_row0 + tm + r
            return carry

        lax.fori_loop(0, tm, fill, 0)

    @pl.when(i > 0)
    def _():
        wait_step_rows()
        pos_copy(prev).wait()

    @pl.when(i < n_steps)
    def _():
        rows_per_chunk = tm // (D_MODEL // OUT_CHUNK)

        ra = _rms_scale(jnp.sum(sa_ref[...], axis=-1, keepdims=True), GMLP_WIDTH)
        rb = _rms_scale(jnp.sum(sb_ref[...], axis=-1, keepdims=True), CONV_CH)
        yna = (ya_ref[...].astype(F32) * ra * ga_ref[...]).astype(BF16)
        ynb = (yb_ref[...].astype(F32) * rb * gb_ref[...]).astype(BF16)
        yn = jnp.concatenate([yna, ynb], axis=1)
        for j in range(D_MODEL // OUT_CHUNK):
            cols = slice(j * OUT_CHUNK, (j + 1) * OUT_CHUNK)
            h_ref[:, cols] = x_ref[:, cols] + jnp.dot(
                yn, wout_ref[:, cols], preferred_element_type=F32)
            for r in range(j * rows_per_chunk, (j + 1) * rows_per_chunk):
                for k in range(2):
                    row_copy(r, k).start()
        h = h_ref[...]

        ms = jnp.mean(h * h, axis=-1, keepdims=True)
        hn = h * lax.rsqrt(ms + EPS) * gffn_ref[...]
        logits = jnp.dot(hn.astype(BF16), wr_ref[...], preferred_element_type=F32) + br_ref[...]

        lane = lax.broadcasted_iota(I32, (tm, LANES), 1)
        lane_f = lane.astype(F32)
        big = float(LANES)

        is_g = (lane >= GROUP_LANE0) & (lane < GROUP_LANE0 + N_GROUPS)
        lg = jnp.where(is_g, logits, MASKED)
        eg = jnp.where(is_g, jnp.exp(lg - jnp.max(lg, axis=-1, keepdims=True)), 0.0)
        pg = eg / jnp.sum(eg, axis=-1, keepdims=True)
        pg_top = jnp.max(pg, axis=-1, keepdims=True)
        g_idx = jnp.min(jnp.where(is_g & (pg == pg_top), lane_f - GROUP_LANE0, big),
                        axis=-1, keepdims=True).astype(I32)

        group_shift = EXPERTS_PER_GROUP.bit_length() - 1
        in_grp = (lane < N_EXPERTS) & ((lane >> group_shift) == g_idx)
        le = jnp.where(in_grp, logits, MASKED)
        ee = jnp.where(in_grp, jnp.exp(le - jnp.max(le, axis=-1, keepdims=True)), 0.0)
        q = ee / jnp.sum(ee, axis=-1, keepdims=True)
        q1 = jnp.max(jnp.where(in_grp, q, -1.0), axis=-1, keepdims=True)
        e1 = jnp.min(jnp.where(in_grp & (q == q1), lane_f, big), axis=-1, keepdims=True)
        rest = in_grp & (lane_f != e1)
        q2 = jnp.max(jnp.where(rest, q, -1.0), axis=-1, keepdims=True)
        e2 = jnp.min(jnp.where(rest & (q == q2), lane_f, big), axis=-1, keepdims=True)
        qs = q1 + q2
        gate1 = pg_top * (q1 / qs)
        gate2 = pg_top * (q2 / qs)

        is1 = lane_f == e1
        is2 = lane_f == e2
        sel = jnp.where(is1 | is2, 1.0, 0.0)
        t_i = lax.broadcasted_iota(I32, (tm, tm), 0)
        t_j = lax.broadcasted_iota(I32, (tm, tm), 1)
        before = jnp.where(t_j < t_i, 1.0, 0.0).astype(BF16)
        run0 = st_scr[0:1, :]
        cur_page = st_scr[1:2, :]
        next_free = st_scr[2:3, :]
        cum = jnp.dot(before, sel.astype(BF16), preferred_element_type=F32) + run0

        lane_row = lax.broadcasted_iota(I32, (1, LANES), 1)
        run1 = run0 + jnp.sum(sel, axis=0, keepdims=True)
        pages0 = jnp.ceil(run0 / EXP_ROWS)
        n_new = jnp.ceil(run1 / EXP_ROWS) - pages0
        first_new = next_free + _lane_cumsum(n_new, lane_row) - n_new

        def pick(mask, per_expert):
            return jnp.sum(jnp.where(mask, per_expert, 0.0), axis=-1, keepdims=True)

        def place(mask):
            rank = pick(mask, cum)
            page_idx = jnp.floor(rank / EXP_ROWS)
            owned = pick(mask, pages0)
            page = jnp.where(page_idx < owned, pick(mask, cur_page),
                             pick(mask, first_new) + (page_idx - owned))
            return page * EXP_ROWS + (rank - page_idx * EXP_ROWS)

        pos1 = place(is1)
        pos2 = place(is2)

        st_scr[0:1, :] = run1
        st_scr[1:2, :] = jnp.where(n_new > 0, first_new + n_new - 1, cur_page)
        st_scr[2:3, :] = next_free + jnp.sum(n_new, axis=-1, keepdims=True)

        per_expert = jnp.concatenate(
            [pages0, n_new, first_new, jnp.zeros((LANES - 3, LANES), F32)], axis=0).T
        owned_c = per_expert[0:N_EXPERTS, 0:1]
        n_new_c = per_expert[0:N_EXPERTS, 1:2]
        first_c = per_expert[0:N_EXPERTS, 2:3]
        idx = lax.broadcasted_iota(I32, (N_EXPERTS, LANES), 1).astype(F32)
        fresh = (idx >= owned_c) & (idx < owned_c + n_new_c)
        ptab_scr[...] = jnp.where(fresh, first_c + (idx - owned_c), ptab_scr[...])
        cnt_vmem[...] = jnp.broadcast_to(run1, cnt_vmem.shape).astype(I32)
        ptab_vmem[...] = ptab_scr[...].astype(I32)

        token = (i * tm + lax.broadcasted_iota(I32, (tm, 1), 0) + 1).astype(F32)
        meta = jnp.zeros((tm, LANES), F32)
        for k, col in ((META_TOKEN, token), (META_GATE0, gate1), (META_POS0, pos1),
                       (META_GATE1, gate2), (META_POS1, pos2)):
            meta = jnp.where(lane == k, col, meta)
        row_buf[slot, :, 0:D_MODEL] = hn
        row_buf[slot, :, D_MODEL:XS_COLS] = meta
        pos_vmem[...] = _lane_dense([pos1, pos2], lane).astype(I32)
        pos_copy(slot).start()

    @pl.when(i == n_steps)
    def _():
        def issue(r, carry):
            for k in range(2):
                row_copy(r, k).start()
            return carry

        lax.fori_loop(0, tm, issue, 0, unroll=8)
        wait_step_rows()

        for vec, scal in ((cnt_vmem, cnt_smem), (ptab_vmem, ptab_smem)):
            copy = pltpu.make_async_copy(vec, scal, fill_sem)
            copy.start()
            copy.wait()
        _finish_pages(trash_row0, cnt_smem, ptab_smem, xs_ref, tab_ref, zero_scr, fill_sem)


def _out_route(ya, yb, sa, sb, x2, ga, gb, w_out, gffn, wr, br, n_page_rows):
    t = x2.shape[0]
    tm = OUT_ROWS
    n_steps = t // tm
    const = lambda i: (0, 0)
    rows = lambda i: (jnp.minimum(i, n_steps - 1), 0)
    return pl.pallas_call(
        functools.partial(_out_route_body, n_page_rows),
        name="out_route",
        grid=(n_steps + 1,),
        in_specs=[
            pl.BlockSpec((tm, GMLP_WIDTH), rows),
            pl.BlockSpec((tm, CONV_CH), rows),
            pl.BlockSpec((tm, LANES), rows),
            pl.BlockSpec((tm, LANES), rows),
            pl.BlockSpec((tm, D_MODEL), rows),
            pl.BlockSpec((1, GMLP_WIDTH), const),
            pl.BlockSpec((1, CONV_CH), const),
            pl.BlockSpec((D_MODEL, D_MODEL), const, pipeline_mode=pl.Buffered(1)),
            pl.BlockSpec((1, D_MODEL), const),
            pl.BlockSpec((D_MODEL, LANES), const),
            pl.BlockSpec((1, LANES), const),
        ],
        out_specs=[
            pl.BlockSpec((tm, D_MODEL), rows),
            pl.BlockSpec(memory_space=pl.ANY),
            pl.BlockSpec(memory_space=pltpu.SMEM),
        ],
        out_shape=[
            jax.ShapeDtypeStruct((t, D_MODEL), F32),
            jax.ShapeDtypeStruct((n_page_rows + 2 * tm, XS_COLS), F32),
            jax.ShapeDtypeStruct((TAB_ROWS, TAB_LANES), I32),
        ],
        scratch_shapes=[
            pltpu.VMEM((2, tm, XS_COLS), F32),
            pltpu.VMEM((SUBLANES, tm), I32),
            pltpu.SMEM((2, SUBLANES, tm), I32),
            pltpu.VMEM((SUBLANES, LANES), F32),
            pltpu.VMEM((N_EXPERTS, LANES), F32),
            pltpu.VMEM((SUBLANES, LANES), I32),
            pltpu.VMEM((N_EXPERTS, LANES), I32),
            pltpu.SMEM((SUBLANES, LANES), I32),
            pltpu.SMEM((N_EXPERTS, LANES), I32),
            pltpu.VMEM((ZERO_ROWS, XS_COLS), F32),
            pltpu.SemaphoreType.DMA,
            pltpu.SemaphoreType.DMA,
            pltpu.SemaphoreType.DMA,
        ],
        compiler_params=pltpu.CompilerParams(
            dimension_semantics=("arbitrary",),
            vmem_limit_bytes=VMEM_LIMIT),
    )(ya, yb, sa, sb, x2, ga, gb, w_out, gffn, wr, br)


def _zero_range_copies(zero_scr, xs_ref, sem, start, length):
    head = length & (SUBLANES - 1)
    copies = []
    for j in range(SUBLANES - 1):
        copy = pltpu.make_async_copy(
            zero_scr.at[pl.ds(0, 1)], xs_ref.at[pl.ds(start + j, 1)], sem)
        copies.append((j < head, copy))
    body_start = start + head
    body = length - head
    size = ZERO_ROWS
    while size >= SUBLANES:
        offset = pl.multiple_of(body_start + (body & ~(2 * size - 1)), SUBLANES)
        copy = pltpu.make_async_copy(
            zero_scr.at[pl.ds(0, size)], xs_ref.at[pl.ds(offset, size)], sem)
        copies.append(((body & size) != 0, copy))
        size //= 2
    return copies


def _expert_mlp_body(n_tokens, tab_ref, xs_ref, wg_hbm, wu_hbm, wd_hbm, out2_ref,
                     wg_f32, wu_f32, wd_f32, wg_scr, wu_scr, wd_scr, out_buf, dest_vmem,
                     dest_smem, row_sem, dest_sem, w_sem):
    s = pl.program_id(0)
    n_pages = tab_ref[TAB_N_PAGES, 0]
    bm = xs_ref.shape[0]
    cur = lax.rem(s, 3)
    prv = lax.rem(s + 2, 3)
    par = lax.rem(s, 2)
    ppar = 1 - par
    trash_row0 = 2 * n_tokens
    expert = tab_ref[TAB_EXPERT, s]
    run_len = tab_ref[TAB_RUN_LEN, s]
    wslot = tab_ref[TAB_WSLOT, s]

    def weight_copies(e, slot):
        return [
            pltpu.make_async_copy(src.at[e], dst.at[slot], w_sem.at[slot])
            for src, dst in ((wg_hbm, wg_f32), (wu_hbm, wu_f32), (wd_hbm, wd_f32))
        ]

    def row_copy(r):
        return pltpu.make_async_copy(
            out_buf.at[prv, r], out2_ref.at[dest_smem[ppar, 0, r]], row_sem.at[prv])

    def wait_rows(slot):
        pltpu.make_async_copy(out_buf.at[0], out2_ref.at[pl.ds(0, bm)], row_sem.at[slot]).wait()

    def dest_copy(p):
        return pltpu.make_async_copy(dest_vmem, dest_smem.at[p], dest_sem)

    @pl.when(s == 0)
    def _():
        out_buf[2] = jnp.zeros(out_buf.shape[1:], F32)
        for half in range(2):
            clear = pltpu.make_async_copy(
                out_buf.at[2], out2_ref.at[pl.ds(trash_row0 + half * bm, bm)], dest_sem)
            clear.start()
            clear.wait()

        def fill(r, carry):
            dest_smem[1, 0, r] = trash_row0 + bm + r
            return carry

        lax.fori_loop(0, bm, fill, 0)
        for copy in weight_copies(expert, wslot):
            copy.start()

    @pl.when((s >= 2) & (s <= n_pages))
    def _():
        wait_rows(cur)

    @pl.when((s >= 1) & (s <= n_pages))
    def _():
        dest_copy(ppar).wait()

    @pl.when((s < n_pages) & (run_len > 0))
    def _():
        for copy in weight_copies(expert, wslot):
            copy.wait()
        wg_scr[...] = wg_f32[wslot].astype(BF16)
        wu_scr[...] = wu_f32[wslot].astype(BF16)
        wd_scr[...] = wd_f32[wslot].astype(BF16)
        next_run = s + run_len

        @pl.when(next_run < n_pages)
        def _():
            for copy in weight_copies(tab_ref[TAB_EXPERT, next_run], 1 - wslot):
                copy.start()

    @pl.when(s < n_pages)
    def _():
        n_groups = 8
        group_rows = bm // n_groups
        half_e = D_EXPERT // 2
        quarter_d = D_MODEL // 4

        def scatter_group(g):
            for r in range(g * group_rows, (g + 1) * group_rows):
                row_copy(r).start()

        x = xs_ref[:, 0:D_MODEL].astype(BF16)
        meta = xs_ref[:, D_MODEL:XS_COLS]
        gate_parts, up_parts = [], []
        for part in range(2):
            cols = slice(part * half_e, (part + 1) * half_e)
            gate_parts.append(jnp.dot(x, wg_scr[:, cols], preferred_element_type=F32))
            scatter_group(2 * part)
            up_parts.append(jnp.dot(x, wu_scr[:, cols], preferred_element_type=F32))
            scatter_group(2 * part + 1)
        gate = jnp.concatenate(gate_parts, axis=1)
        up = jnp.concatenate(up_parts, axis=1)
        hidden = (jax.nn.silu(gate) * up).astype(BF16)

        own_row = (tab_ref[TAB_ORDER, s] * bm
                   + lax.broadcasted_iota(I32, (bm, 1), 0)).astype(F32)
        local = lax.broadcasted_iota(I32, (bm, 1), 0).astype(F32)
        token = meta[:, META_TOKEN:META_TOKEN + 1]
        first = meta[:, META_POS0:META_POS0 + 1] == own_row
        weight = jnp.where(first, meta[:, META_GATE0:META_GATE0 + 1],
                           meta[:, META_GATE1:META_GATE1 + 1])
        plane = jnp.where(first, 0.0, float(n_tokens))
        dest = jnp.where(token > 0.0, plane + token - 1.0,
                         (trash_row0 + par * bm).astype(F32) + local)

        for part in range(4):
            cols = slice(part * quarter_d, (part + 1) * quarter_d)
            rows = jnp.dot(hidden, wd_scr[:, cols], preferred_element_type=F32) * weight
            for c in range(quarter_d // LANES):
                out_buf[cur, :, part * (quarter_d // LANES) + c, :] = (
                    rows[:, c * LANES:(c + 1) * LANES])
            scatter_group(4 + part)

        lane = lax.broadcasted_iota(I32, (bm, LANES), 1)
        dest_vmem[...] = _lane_dense([dest], lane).astype(I32)
        dest_copy(par).start()

    @pl.when(s == n_pages)
    def _():
        def issue(r, carry):
            row_copy(r).start()
            return carry

        lax.fori_loop(0, bm, issue, 0, unroll=8)
        wait_rows(lax.rem(s + 1, 3))
        wait_rows(prv)


def _expert_mlp(tab, xs, w_gate, w_up, w_down, n_tokens, n_pages_max):
    bm = EXP_ROWS
    page_of = lambda s, tab: (tab[TAB_ORDER, s], 0)
    return pl.pallas_call(
        functools.partial(_expert_mlp_body, n_tokens),
        name="expert_mlp",
        grid_spec=pltpu.PrefetchScalarGridSpec(
            num_scalar_prefetch=1,
            grid=(n_pages_max + 1,),
            in_specs=[
                pl.BlockSpec((bm, XS_COLS), page_of),
                pl.BlockSpec(memory_space=pl.ANY),
                pl.BlockSpec(memory_space=pl.ANY),
                pl.BlockSpec(memory_space=pl.ANY),
            ],
            out_specs=pl.BlockSpec(memory_space=pl.ANY),
            scratch_shapes=[
                pltpu.VMEM((2, D_MODEL, D_EXPERT), F32),
                pltpu.VMEM((2, D_MODEL, D_EXPERT), F32),
                pltpu.VMEM((2, D_EXPERT, D_MODEL), F32),
                pltpu.VMEM((D_MODEL, D_EXPERT), BF16),
                pltpu.VMEM((D_MODEL, D_EXPERT), BF16),
                pltpu.VMEM((D_EXPERT, D_MODEL), BF16),
                pltpu.VMEM((3, bm, ROW_TILES, LANES), F32),
                pltpu.VMEM((SUBLANES, bm), I32),
                pltpu.SMEM((2, SUBLANES, bm), I32),
                pltpu.SemaphoreType.DMA((3,)),
                pltpu.SemaphoreType.DMA,
                pltpu.SemaphoreType.DMA((2,)),
            ],
        ),
        out_shape=jax.ShapeDtypeStruct((2 * n_tokens + 2 * bm, ROW_TILES, LANES), F32),
        compiler_params=pltpu.CompilerParams(
            dimension_semantics=("arbitrary",),
            vmem_limit_bytes=VMEM_LIMIT),
    )(tab, xs, w_gate, w_up, w_down)


def _final_norm_body(h_ref, y0_ref, y1_ref, gfin_ref, out_ref):
    rows = lambda ref: jnp.concatenate([ref[:, c, :] for c in range(ROW_TILES)], axis=1)
    h = h_ref[...] + (rows(y0_ref) + rows(y1_ref))
    ms = jnp.mean(h * h, axis=-1, keepdims=True)
    out_ref[...] = h * lax.rsqrt(ms + EPS) * gfin_ref[...]


def _final_norm(h, out2, gfin):
    t = h.shape[0]
    tm = FIN_ROWS
    plane1 = t // tm
    return pl.pallas_call(
        _final_norm_body,
        name="final_norm",
        grid=(t // tm,),
        in_specs=[
            pl.BlockSpec((tm, D_MODEL), lambda i: (i, 0)),
            pl.BlockSpec((tm, ROW_TILES, LANES), lambda i: (i, 0, 0)),
            pl.BlockSpec((tm, ROW_TILES, LANES), lambda i: (i + plane1, 0, 0)),
            pl.BlockSpec((1, D_MODEL), lambda i: (0, 0)),
        ],
        out_specs=pl.BlockSpec((tm, D_MODEL), lambda i: (i, 0)),
        out_shape=jax.ShapeDtypeStruct((t, D_MODEL), F32),
        compiler_params=pltpu.CompilerParams(
            dimension_semantics=("arbitrary",),
            vmem_limit_bytes=VMEM_LIMIT),
    )(h, out2, out2, gfin)


def kernel(x, norm_mix_g, w_in, gmlp_v_norm_g, gmlp_ws, gmlp_bs, conv_w, out_norm_gmlp_g,
           out_norm_conv_g, w_out, norm_ffn_g, router_group_w, router_group_b, router_expert_w,
           router_expert_b, expert_w_gate, expert_w_up, expert_w_down, norm_final_g):
    batch, seq_len, d_model = x.shape
    t = batch * seq_len
    assert w_in.shape[0] == 1, "single-layer block"
    assert t // EXP_ROWS <= LANES, "page table holds at most LANES pages per expert"
    n_col_groups = GMLP_WIDTH // MIX_COLS
    n_pages_max = 2 * t // EXP_ROWS + N_EXPERTS
    n_page_rows = n_pages_max * EXP_ROWS
    x2 = x.reshape(t, d_model)

    vng = gmlp_v_norm_g[0].reshape(n_col_groups, 1, MIX_COLS)
    bsb = jnp.broadcast_to(gmlp_bs[0][:, :, None], (GMLP_HEADS, GMLP_BLOCK, HEAD_DIM))
    taps = conv_w[0].reshape(CONV_K, n_col_groups, MIX_COLS).transpose(1, 0, 2)
    unused = LANES - N_EXPERTS - N_GROUPS
    wr = jnp.concatenate(
        [router_expert_w[0], router_group_w[0], jnp.zeros((d_model, unused), F32)],
        axis=1).astype(BF16)
    br = jnp.concatenate(
        [router_expert_b[0], router_group_b[0], jnp.zeros((unused,), F32)])[None, :]

    ya, yb, sa, sb = _mixer_proj(x2, norm_mix_g[0][None, :], w_in[0].astype(BF16), vng,
                                 gmlp_ws[0], bsb, taps, seq_len)
    h, xs, tab = _out_route(
        ya, yb, sa, sb, x2, out_norm_gmlp_g[0][None, :], out_norm_conv_g[0][None, :],
        w_out[0].astype(BF16), norm_ffn_g[0][None, :], wr, br, n_page_rows)
    out2 = _expert_mlp(tab, xs, expert_w_gate[0], expert_w_up[0], expert_w_down[0],
                       t, n_pages_max)
    out = _final_norm(h, out2, norm_final_g[None, :])
    return out.reshape(batch, seq_len, d_model)
```

```python
import functools

import jax
import jax.numpy as jnp
from jax import lax
from jax.experimental import pallas as pl
from jax.experimental.pallas import tpu as pltpu

F32 = jnp.float32
BF16 = jnp.bfloat16
I32 = jnp.int32

D_MODEL = 2048
CHUNK = 64
GMLP_WIDTH = 1024
GMLP_HEADS = 8
HEAD_DIM = 128
GMLP_BLOCK = 128
CONV_CH = 1024
CONV_K = 3
N_GROUPS = 4
EXPERTS_PER_GROUP = 8
N_EXPERTS = 32
D_EXPERT = 512
EPS = 1e-6

LANES = 128
SUBLANES = 8

MIX_ROWS = 1024
MIX_COLS = 256
OUT_ROWS = 512
OUT_CHUNK = 256
EXP_ROWS = 256
FIN_ROWS = 512
ZERO_ROWS = 128
GROUP_LANE0 = N_EXPERTS
MASKED = -1e30
VMEM_LIMIT = 56 * 1024 * 1024

XS_COLS = D_MODEL + LANES
META_TOKEN = 0
META_GATE0 = 1
META_POS0 = 2
META_GATE1 = 3
META_POS1 = 4

TAB_ORDER = 0
TAB_EXPERT = 1
TAB_N_PAGES = 2
TAB_RUN_LEN = 3
TAB_WSLOT = 4
TAB_ROWS = 8
TAB_LANES = 256


def _rms_scale(sumsq, width):
    return lax.rsqrt(sumsq / width + EPS)


def _mixer_proj_body(seq_len, x_ref, gmix_ref, wu_ref, wv_ref, wbg_ref, wcg_ref, whv_ref,
                     vng_ref, ws_ref, bsb_ref, cw_ref,
                     ya_ref, yb_ref, sa_ref, sb_ref, xn_scr, carry_scr):
    i = pl.program_id(0)
    c = pl.program_id(1)
    tm = x_ref.shape[0]
    cw = ya_ref.shape[1]
    heads_per_step = cw // HEAD_DIM

    @pl.when(c == 0)
    def _():
        x = x_ref[...]
        ms = jnp.mean(x * x, axis=-1, keepdims=True)
        xn_scr[...] = (x * lax.rsqrt(ms + EPS) * gmix_ref[...]).astype(BF16)
        sa_ref[...] = jnp.zeros_like(sa_ref)
        sb_ref[...] = jnp.zeros_like(sb_ref)

    xn = xn_scr[...]
    project = lambda w_ref: jnp.dot(xn, w_ref[...], preferred_element_type=F32)
    u = jax.nn.gelu(project(wu_ref))
    v = jax.nn.gelu(project(wv_ref))
    bg = project(wbg_ref)
    cg = project(wcg_ref)
    hv = project(whv_ref)

    pos_i = lax.broadcasted_iota(I32, (GMLP_BLOCK, GMLP_BLOCK), 0)
    pos_j = lax.broadcasted_iota(I32, (GMLP_BLOCK, GMLP_BLOCK), 1)
    chunk_shift = CHUNK.bit_length() - 1
    causal = (pos_i >> chunk_shift) >= (pos_j >> chunk_shift)
    vng = vng_ref[0]
    ya_heads = []
    for j in range(heads_per_step):
        head = c * heads_per_step + j
        lanes = slice(j * HEAD_DIM, (j + 1) * HEAD_DIM)
        vj = v[:, lanes]
        ms = jnp.mean(vj * vj, axis=-1, keepdims=True)
        vn = (vj * lax.rsqrt(ms + EPS) * vng[:, lanes]).astype(BF16)
        w_mix = jnp.where(causal, ws_ref[head], 0.0).astype(BF16)
        bias = bsb_ref[head]
        mixed = [
            jnp.dot(w_mix, vn[p * GMLP_BLOCK:(p + 1) * GMLP_BLOCK, :],
                    preferred_element_type=F32) + bias
            for p in range(tm // GMLP_BLOCK)
        ]
        ya_heads.append(u[:, lanes] * jnp.concatenate(mixed, axis=0))
    ya = jnp.concatenate(ya_heads, axis=1) if heads_per_step > 1 else ya_heads[0]

    z = cg * hv
    prev = carry_scr[c]
    seq_start = (i * tm) % seq_len == 0
    prev = jnp.where(seq_start, 0.0, prev)
    row = lax.broadcasted_iota(I32, (tm, cw), 0)
    z1 = jnp.where(row == 0, prev[SUBLANES - 1:SUBLANES, :], pltpu.roll(z, 1, axis=0))
    z2 = jnp.where(row == 0, prev[SUBLANES - 2:SUBLANES - 1, :],
                   jnp.where(row == 1, prev[SUBLANES - 1:SUBLANES, :],
                             pltpu.roll(z, 2, axis=0)))
    taps = cw_ref[0]
    conv = taps[0:1, :] * z2 + taps[1:2, :] * z1 + taps[2:3, :] * z
    yb = bg * conv
    carry_scr[c] = z[tm - SUBLANES:tm, :]

    ya_ref[...] = ya.astype(BF16)
    yb_ref[...] = yb.astype(BF16)
    ya2 = ya * ya
    yb2 = yb * yb
    sa_ref[...] += sum(ya2[:, k * LANES:(k + 1) * LANES] for k in range(cw // LANES))
    sb_ref[...] += sum(yb2[:, k * LANES:(k + 1) * LANES] for k in range(cw // LANES))


def _mixer_proj(x2, gmix, w_in, vng, ws, bsb, conv_taps, seq_len):
    t = x2.shape[0]
    cw = MIX_COLS
    tm = MIX_ROWS
    n_groups = GMLP_WIDTH // cw
    grid = (t // tm, n_groups)
    w_part = lambda k: pl.BlockSpec((D_MODEL, cw), lambda i, c: (0, k * n_groups + c))
    return pl.pallas_call(
        functools.partial(_mixer_proj_body, seq_len),
        name="mixer_proj",
        grid=grid,
        in_specs=[
            pl.BlockSpec((tm, D_MODEL), lambda i, c: (i, 0)),
            pl.BlockSpec((1, D_MODEL), lambda i, c: (0, 0)),
            w_part(0), w_part(1), w_part(2), w_part(3), w_part(4),
            pl.BlockSpec((1, 1, cw), lambda i, c: (c, 0, 0)),
            pl.BlockSpec((GMLP_HEADS, GMLP_BLOCK, GMLP_BLOCK), lambda i, c: (0, 0, 0)),
            pl.BlockSpec((GMLP_HEADS, GMLP_BLOCK, HEAD_DIM), lambda i, c: (0, 0, 0)),
            pl.BlockSpec((1, CONV_K, cw), lambda i, c: (c, 0, 0)),
        ],
        out_specs=[
            pl.BlockSpec((tm, cw), lambda i, c: (i, c)),
            pl.BlockSpec((tm, cw), lambda i, c: (i, c)),
            pl.BlockSpec((tm, LANES), lambda i, c: (i, 0)),
            pl.BlockSpec((tm, LANES), lambda i, c: (i, 0)),
        ],
        out_shape=[
            jax.ShapeDtypeStruct((t, GMLP_WIDTH), BF16),
            jax.ShapeDtypeStruct((t, CONV_CH), BF16),
            jax.ShapeDtypeStruct((t, LANES), F32),
            jax.ShapeDtypeStruct((t, LANES), F32),
        ],
        scratch_shapes=[
            pltpu.VMEM((tm, D_MODEL), BF16),
            pltpu.VMEM((n_groups, SUBLANES, cw), F32),
        ],
        compiler_params=pltpu.CompilerParams(
            dimension_semantics=("arbitrary", "arbitrary"),
            vmem_limit_bytes=VMEM_LIMIT),
    )(x2, gmix, w_in, w_in, w_in, w_in, w_in, vng, ws, bsb, conv_taps)


def _lane_cumsum(v, lane):
    shift = 1
    while shift < N_EXPERTS:
        v = v + jnp.where(lane >= shift, pltpu.roll(v, shift, axis=1), 0.0)
        shift *= 2
    return v


def _lane_dense(col_values, lane):
    tile = jnp.zeros(lane.shape, F32)
    for k, col in enumerate(col_values):
        tile = jnp.where(lane == k, col, tile)
    return tile.T[0:SUBLANES, :]


def _finish_pages(n_page_rows, cnt_ref, ptab_ref, xs_ref, tab_ref, zero_scr, sem):
    page_shift = EXP_ROWS.bit_length() - 1
    n_slots = tab_ref.shape[1]
    zero_scr[...] = jnp.zeros_like(zero_scr)

    def clear(s, carry):
        for row in range(TAB_ROWS):
            tab_ref[row, s] = 0
        return carry

    lax.fori_loop(0, n_slots, clear, 0)

    def pages_of(e):
        return (cnt_ref[0, e] + (EXP_ROWS - 1)) >> page_shift

    def list_pages(e, carry):
        first_slot, run = carry

        def put(j, carry):
            tab_ref[TAB_ORDER, first_slot + j] = ptab_ref[e, j]
            tab_ref[TAB_EXPERT, first_slot + j] = e
            tab_ref[TAB_WSLOT, first_slot + j] = run & 1
            return carry

        n_pages_e = pages_of(e)
        lax.fori_loop(0, n_pages_e, put, 0)

        @pl.when(n_pages_e > 0)
        def _():
            tab_ref[TAB_RUN_LEN, first_slot] = n_pages_e

        return first_slot + n_pages_e, run + jnp.where(n_pages_e > 0, 1, 0)

    n_pages, _ = lax.fori_loop(0, N_EXPERTS, list_pages, (0, 0))
    tab_ref[TAB_N_PAGES, 0] = n_pages
    last_page = tab_ref[TAB_ORDER, n_pages - 1]
    last_expert = tab_ref[TAB_EXPERT, n_pages - 1]

    def repeat_last(s, carry):
        tab_ref[TAB_ORDER, s] = last_page
        tab_ref[TAB_EXPERT, s] = last_expert
        return carry

    lax.fori_loop(n_pages, n_slots, repeat_last, 0)

    def pad_copies(e):
        used = cnt_ref[0, e] & (EXP_ROWS - 1)
        last = ptab_ref[e, jnp.maximum(pages_of(e) - 1, 0)]
        return _zero_range_copies(zero_scr, xs_ref, sem, last * EXP_ROWS + used,
                                  (EXP_ROWS - used) & (EXP_ROWS - 1))

    def tail_copies(p):
        return [
            pltpu.make_async_copy(
                zero_scr,
                xs_ref.at[pl.ds(pl.multiple_of(p * EXP_ROWS + part * ZERO_ROWS, ZERO_ROWS),
                                ZERO_ROWS)],
                sem)
            for part in range(EXP_ROWS // ZERO_ROWS)
        ]

    def start_pads(e, carry):
        for needed, copy in pad_copies(e):
            pl.when(needed)(copy.start)
        return carry

    def wait_pads(e, carry):
        for needed, copy in pad_copies(e):
            pl.when(needed)(copy.wait)
        return carry

    def start_tail(p, carry):
        for copy in tail_copies(p):
            copy.start()
        return carry

    def wait_tail(p, carry):
        for copy in tail_copies(p):
            copy.wait()
        return carry

    total_pages = n_page_rows // EXP_ROWS
    lax.fori_loop(0, N_EXPERTS, start_pads, 0)
    lax.fori_loop(n_pages, total_pages, start_tail, 0)
    lax.fori_loop(0, N_EXPERTS, wait_pads, 0)
    lax.fori_loop(n_pages, total_pages, wait_tail, 0)


def _out_route_body(trash_row0, ya_ref, yb_ref, sa_ref, sb_ref, x_ref, ga_ref, gb_ref, wout_ref,
                    gffn_ref, wr_ref, br_ref, h_ref, xs_ref, tab_ref,
                    row_buf, pos_vmem, pos_smem, st_scr, ptab_scr, cnt_vmem, ptab_vmem,
                    cnt_smem, ptab_smem, zero_scr, row_sem, pos_sem, fill_sem):
    i = pl.program_id(0)
    n_steps = pl.num_programs(0) - 1
    tm = x_ref.shape[0]
    slot = lax.rem(i, 2)
    prev = 1 - slot

    def row_copy(r, k):
        return pltpu.make_async_copy(
            row_buf.at[prev, pl.ds(r, 1)], xs_ref.at[pl.ds(pos_smem[prev, k, r], 1)], row_sem)

    def wait_step_rows():
        for _ in range(2):
            pltpu.make_async_copy(row_buf.at[0], xs_ref.at[pl.ds(0, tm)], row_sem).wait()

    def pos_copy(s):
        return pltpu.make_async_copy(pos_vmem, pos_smem.at[s], pos_sem)

    @pl.when(i == 0)
    def _():
        st_scr[...] = jnp.zeros_like(st_scr)
        ptab_scr[...] = jnp.zeros_like(ptab_scr)
        row_buf[1] = jnp.zeros((tm, XS_COLS), F32)
        which = lax.broadcasted_iota(I32, (SUBLANES, tm), 0)
        pos_vmem[...] = (trash_row0 + lax.broadcasted_iota(I32, (SUBLANES, tm), 1)
                         + jnp.where(which == 1, tm, 0))
        pos_copy(1).start()

    @pl.when(i > 0)
    def _():
        wait_step_rows()

    @pl.when(i < n_steps)
    def _():
        rows_per_chunk = tm // (D_MODEL // OUT_CHUNK)

        ra = _rms_scale(jnp.sum(sa_ref[...], axis=-1, keepdims=True), GMLP_WIDTH)
        rb = _rms_scale(jnp.sum(sb_ref[...], axis=-1, keepdims=True), CONV_CH)
        yna = (ya_ref[...].astype(F32) * ra * ga_ref[...]).astype(BF16)
        ynb = (yb_ref[...].astype(F32) * rb * gb_ref[...]).astype(BF16)
        yn = jnp.concatenate([yna, ynb], axis=1)
        for j in range(D_MODEL // OUT_CHUNK):
            cols = slice(j * OUT_CHUNK, (j + 1) * OUT_CHUNK)
            h_ref[:, cols] = x_ref[:, cols] + jnp.dot(
                yn, wout_ref[:, cols], preferred_element_type=F32)
            if j == 0:
                pos_copy(prev).wait()
            for r in range(j * rows_per_chunk, (j + 1) * rows_per_chunk):
                for k in range(2):
                    row_copy(r, k).start(priority=k)
        h = h_ref[...]

        ms = jnp.mean(h * h, axis=-1, keepdims=True)
        hn = h * lax.rsqrt(ms + EPS) * gffn_ref[...]
        logits = jnp.dot(hn.astype(BF16), wr_ref[...], preferred_element_type=F32) + br_ref[...]

        lane = lax.broadcasted_iota(I32, (tm, LANES), 1)
        lane_f = lane.astype(F32)
        big = float(LANES)

        is_g = (lane >= GROUP_LANE0) & (lane < GROUP_LANE0 + N_GROUPS)
        lg = jnp.where(is_g, logits, MASKED)
        eg = jnp.where(is_g, jnp.exp(lg - jnp.max(lg, axis=-1, keepdims=True)), 0.0)
        pg = eg / jnp.sum(eg, axis=-1, keepdims=True)
        pg_top = jnp.max(pg, axis=-1, keepdims=True)
        g_idx = jnp.min(jnp.where(is_g & (pg == pg_top), lane_f - GROUP_LANE0, big),
                        axis=-1, keepdims=True).astype(I32)

        group_shift = EXPERTS_PER_GROUP.bit_length() - 1
        in_grp = (lane < N_EXPERTS) & ((lane >> group_shift) == g_idx)
        le = jnp.where(in_grp, logits, MASKED)
        ee = jnp.where(in_grp, jnp.exp(le - jnp.max(le, axis=-1, keepdims=True)), 0.0)
        q = ee / jnp.sum(ee, axis=-1, keepdims=True)
        q1 = jnp.max(jnp.where(in_grp, q, -1.0), axis=-1, keepdims=True)
        e1 = jnp.min(jnp.where(in_grp & (q == q1), lane_f, big), axis=-1, keepdims=True)
        rest = in_grp & (lane_f != e1)
        q2 = jnp.max(jnp.where(rest, q, -1.0), axis=-1, keepdims=True)
        e2 = jnp.min(jnp.where(rest & (q == q2), lane_f, big), axis=-1, keepdims=True)
        qs = q1 + q2
        gate1 = pg_top * (q1 / qs)
        gate2 = pg_top * (q2 / qs)

        is1 = lane_f == e1
        is2 = lane_f == e2
        sel = jnp.where(is1 | is2, 1.0, 0.0)
        t_i = lax.broadcasted_iota(I32, (tm, tm), 0)
        t_j = lax.broadcasted_iota(I32, (tm, tm), 1)
        before = jnp.where(t_j < t_i, 1.0, 0.0).astype(BF16)
        run0 = st_scr[0:1, :]
        cur_page = st_scr[1:2, :]
        next_free = st_scr[2:3, :]
        cum = jnp.dot(before, sel.astype(BF16), preferred_element_type=F32) + run0

        lane_row = lax.broadcasted_iota(I32, (1, LANES), 1)
        run1 = run0 + jnp.sum(sel, axis=0, keepdims=True)
        pages0 = jnp.ceil(run0 / EXP_ROWS)
        n_new = jnp.ceil(run1 / EXP_ROWS) - pages0
        first_new = next_free + _lane_cumsum(n_new, lane_row) - n_new

        def pick(mask, per_expert):
            return jnp.sum(jnp.where(mask, per_expert, 0.0), axis=-1, keepdims=True)

        def place(mask):
            rank = pick(mask, cum)
            page_idx = jnp.floor(rank / EXP_ROWS)
            owned = pick(mask, pages0)
            page = jnp.where(page_idx < owned, pick(mask, cur_page),
                             pick(mask, first_new) + (page_idx - owned))
            return page * EXP_ROWS + (rank - page_idx * EXP_ROWS)

        pos1 = place(is1)
        pos2 = place(is2)

        st_scr[0:1, :] = run1
        st_scr[1:2, :] = jnp.where(n_new > 0, first_new + n_new - 1, cur_page)
        st_scr[2:3, :] = next_free + jnp.sum(n_new, axis=-1, keepdims=True)

        per_expert = jnp.concatenate(
            [pages0, n_new, first_new, jnp.zeros((LANES - 3, LANES), F32)], axis=0).T
        owned_c = per_expert[0:N_EXPERTS, 0:1]
        n_new_c = per_expert[0:N_EXPERTS, 1:2]
        first_c = per_expert[0:N_EXPERTS, 2:3]
        idx = lax.broadcasted_iota(I32, (N_EXPERTS, LANES), 1).astype(F32)
        fresh = (idx >= owned_c) & (idx < owned_c + n_new_c)
        ptab_scr[...] = jnp.where(fresh, first_c + (idx - owned_c), ptab_scr[...])
        cnt_vmem[...] = jnp.broadcast_to(run1, cnt_vmem.shape).astype(I32)
        ptab_vmem[...] = ptab_scr[...].astype(I32)

        token = (i * tm + lax.broadcasted_iota(I32, (tm, 1), 0) + 1).astype(F32)
        meta = jnp.zeros((tm, LANES), F32)
        for k, col in ((META_TOKEN, token), (META_GATE0, gate1), (META_POS0, pos1),
                       (META_GATE1, gate2), (META_POS1, pos2)):
            meta = jnp.where(lane == k, col, meta)
        row_buf[slot, :, 0:D_MODEL] = hn
        row_buf[slot, :, D_MODEL:XS_COLS] = meta
        pos_vmem[...] = _lane_dense([pos1, pos2], lane).astype(I32)
        pos_copy(slot).start()

    @pl.when(i == n_steps)
    def _():
        def issue(r, carry):
            for k in range(2):
                row_copy(r, k).start(priority=k)
            return carry

        pos_copy(prev).wait()
        lax.fori_loop(0, tm, issue, 0, unroll=8)
        wait_step_rows()

        for vec, scal in ((cnt_vmem, cnt_smem), (ptab_vmem, ptab_smem)):
            copy = pltpu.make_async_copy(vec, scal, fill_sem)
            copy.start()
            copy.wait()
        _finish_pages(trash_row0, cnt_smem, ptab_smem, xs_ref, tab_ref, zero_scr, fill_sem)


def _out_route(ya, yb, sa, sb, x2, ga, gb, w_out, gffn, wr, br, n_page_rows):
    t = x2.shape[0]
    tm = OUT_ROWS
    n_steps = t // tm
    const = lambda i: (0, 0)
    rows = lambda i: (jnp.minimum(i, n_steps - 1), 0)
    return pl.pallas_call(
        functools.partial(_out_route_body, n_page_rows),
        name="out_route",
        grid=(n_steps + 1,),
        in_specs=[
            pl.BlockSpec((tm, GMLP_WIDTH), rows),
            pl.BlockSpec((tm, CONV_CH), rows),
            pl.BlockSpec((tm, LANES), rows),
            pl.BlockSpec((tm, LANES), rows),
            pl.BlockSpec((tm, D_MODEL), rows),
            pl.BlockSpec((1, GMLP_WIDTH), const),
            pl.BlockSpec((1, CONV_CH), const),
            pl.BlockSpec((D_MODEL, D_MODEL), const, pipeline_mode=pl.Buffered(1)),
            pl.BlockSpec((1, D_MODEL), const),
            pl.BlockSpec((D_MODEL, LANES), const),
            pl.BlockSpec((1, LANES), const),
        ],
        out_specs=[
            pl.BlockSpec((tm, D_MODEL), rows),
            pl.BlockSpec(memory_space=pl.ANY),
            pl.BlockSpec(memory_space=pltpu.SMEM),
        ],
        out_shape=[
            jax.ShapeDtypeStruct((t, D_MODEL), F32),
            jax.ShapeDtypeStruct((n_page_rows + 2 * tm, XS_COLS), F32),
            jax.ShapeDtypeStruct((TAB_ROWS, TAB_LANES), I32),
        ],
        scratch_shapes=[
            pltpu.VMEM((2, tm, XS_COLS), F32),
            pltpu.VMEM((SUBLANES, tm), I32),
            pltpu.SMEM((2, SUBLANES, tm), I32),
            pltpu.VMEM((SUBLANES, LANES), F32),
            pltpu.VMEM((N_EXPERTS, LANES), F32),
            pltpu.VMEM((SUBLANES, LANES), I32),
            pltpu.VMEM((N_EXPERTS, LANES), I32),
            pltpu.SMEM((SUBLANES, LANES), I32),
            pltpu.SMEM((N_EXPERTS, LANES), I32),
            pltpu.VMEM((ZERO_ROWS, XS_COLS), F32),
            pltpu.SemaphoreType.DMA,
            pltpu.SemaphoreType.DMA,
            pltpu.SemaphoreType.DMA,
        ],
        compiler_params=pltpu.CompilerParams(
            dimension_semantics=("arbitrary",),
            vmem_limit_bytes=VMEM_LIMIT),
    )(ya, yb, sa, sb, x2, ga, gb, w_out, gffn, wr, br)


def _zero_range_copies(zero_scr, xs_ref, sem, start, length):
    head = length & (SUBLANES - 1)
    copies = []
    for j in range(SUBLANES - 1):
        copy = pltpu.make_async_copy(
            zero_scr.at[pl.ds(0, 1)], xs_ref.at[pl.ds(start + j, 1)], sem)
        copies.append((j < head, copy))
    body_start = start + head
    body = length - head
    size = ZERO_ROWS
    while size >= SUBLANES:
        offset = pl.multiple_of(body_start + (body & ~(2 * size - 1)), SUBLANES)
        copy = pltpu.make_async_copy(
            zero_scr.at[pl.ds(0, size)], xs_ref.at[pl.ds(offset, size)], sem)
        copies.append(((body & size) != 0, copy))
        size //= 2
    return copies


def _expert_mlp_body(n_tokens, tab_ref, xs_ref, wg_hbm, wu_hbm, wd_hbm, out2_ref,
                     wg_f32, wu_f32, wd_f32, wg_scr, wu_scr, wd_scr, out_buf, dest_vmem,
                     dest_smem, row_sem, dest_sem, w_sem):
    s = pl.program_id(0)
    n_pages = tab_ref[TAB_N_PAGES, 0]
    bm = xs_ref.shape[0]
    cur = lax.rem(s, 3)
    prv = lax.rem(s + 2, 3)
    par = lax.rem(s, 2)
    ppar = 1 - par
    trash_row0 = 2 * n_tokens
    expert = tab_ref[TAB_EXPERT, s]
    run_len = tab_ref[TAB_RUN_LEN, s]
    wslot = tab_ref[TAB_WSLOT, s]

    def weight_copies(e, slot):
        return [
            pltpu.make_async_copy(src.at[e], dst.at[slot], w_sem.at[slot])
            for src, dst in ((wg_hbm, wg_f32), (wu_hbm, wu_f32), (wd_hbm, wd_f32))
        ]

    def row_copy(r):
        return pltpu.make_async_copy(
            out_buf.at[prv, pl.ds(r, 1)], out2_ref.at[pl.ds(dest_smem[ppar, 0, r], 1)],
            row_sem.at[prv])

    def wait_rows(slot):
        pltpu.make_async_copy(out_buf.at[0], out2_ref.at[pl.ds(0, bm)], row_sem.at[slot]).wait()

    def dest_copy(p):
        return pltpu.make_async_copy(dest_vmem, dest_smem.at[p], dest_sem)

    @pl.when(s == 0)
    def _():
        out_buf[2] = jnp.zeros((bm, D_MODEL), F32)
        for half in range(2):
            clear = pltpu.make_async_copy(
                out_buf.at[2], out2_ref.at[pl.ds(trash_row0 + half * bm, bm)], dest_sem)
            clear.start()
            clear.wait()

        def fill(r, carry):
            dest_smem[1, 0, r] = trash_row0 + bm + r
            return carry

        lax.fori_loop(0, bm, fill, 0)
        for copy in weight_copies(expert, wslot):
            copy.start()

    @pl.when((s >= 2) & (s <= n_pages))
    def _():
        wait_rows(cur)

    @pl.when((s >= 1) & (s <= n_pages))
    def _():
        dest_copy(ppar).wait()

    @pl.when((s < n_pages) & (run_len > 0))
    def _():
        for copy in weight_copies(expert, wslot):
            copy.wait()
        wg_scr[...] = wg_f32[wslot].astype(BF16)
        wu_scr[...] = wu_f32[wslot].astype(BF16)
        wd_scr[...] = wd_f32[wslot].astype(BF16)
        next_run = s + run_len

        @pl.when(next_run < n_pages)
        def _():
            for copy in weight_copies(tab_ref[TAB_EXPERT, next_run], 1 - wslot):
                copy.start()

    @pl.when(s < n_pages)
    def _():
        n_groups = 8
        group_rows = bm // n_groups
        half_e = D_EXPERT // 2
        quarter_d = D_MODEL // 4

        def scatter_group(g):
            for r in range(g * group_rows, (g + 1) * group_rows):
                row_copy(r).start(priority=r % 2)

        meta = xs_ref[:, D_MODEL:XS_COLS]
        own_row = (tab_ref[TAB_ORDER, s] * bm
                   + lax.broadcasted_iota(I32, (bm, 1), 0)).astype(F32)
        local = lax.broadcasted_iota(I32, (bm, 1), 0).astype(F32)
        token = meta[:, META_TOKEN:META_TOKEN + 1]
        first = meta[:, META_POS0:META_POS0 + 1] == own_row
        weight = jnp.where(first, meta[:, META_GATE0:META_GATE0 + 1],
                           meta[:, META_GATE1:META_GATE1 + 1])
        plane = jnp.where(first, 0.0, float(n_tokens))
        dest = jnp.where(token > 0.0, plane + token - 1.0,
                         (trash_row0 + par * bm).astype(F32) + local)
        lane = lax.broadcasted_iota(I32, (bm, LANES), 1)
        dest_vmem[...] = _lane_dense([dest], lane).astype(I32)
        dest_copy(par).start()

        x = xs_ref[:, 0:D_MODEL].astype(BF16)
        gate_parts, up_parts = [], []
        for part in range(2):
            cols = slice(part * half_e, (part + 1) * half_e)
            gate_parts.append(jnp.dot(x, wg_scr[:, cols], preferred_element_type=F32))
            scatter_group(2 * part)
            up_parts.append(jnp.dot(x, wu_scr[:, cols], preferred_element_type=F32))
            scatter_group(2 * part + 1)
        gate = jnp.concatenate(gate_parts, axis=1)
        up = jnp.concatenate(up_parts, axis=1)
        hidden = (jax.nn.silu(gate) * up).astype(BF16)

        for part in range(4):
            cols = slice(part * quarter_d, (part + 1) * quarter_d)
            out_buf[cur, :, cols] = jnp.dot(
                hidden, wd_scr[:, cols], preferred_element_type=F32) * weight
            scatter_group(4 + part)

    @pl.when(s == n_pages)
    def _():
        def issue(pair, carry):
            for k in range(2):
                row_copy(2 * pair + k).start(priority=k)
            return carry

        lax.fori_loop(0, bm // 2, issue, 0, unroll=4)
        wait_rows(lax.rem(s + 1, 3))
        wait_rows(prv)


def _expert_mlp(tab, xs, w_gate, w_up, w_down, n_tokens, n_pages_max):
    bm = EXP_ROWS
    page_of = lambda s, tab: (tab[TAB_ORDER, s], 0)
    return pl.pallas_call(
        functools.partial(_expert_mlp_body, n_tokens),
        name="expert_mlp",
        grid_spec=pltpu.PrefetchScalarGridSpec(
            num_scalar_prefetch=1,
            grid=(n_pages_max + 1,),
            in_specs=[
                pl.BlockSpec((bm, XS_COLS), page_of),
                pl.BlockSpec(memory_space=pl.ANY),
                pl.BlockSpec(memory_space=pl.ANY),
                pl.BlockSpec(memory_space=pl.ANY),
            ],
            out_specs=pl.BlockSpec(memory_space=pl.ANY),
            scratch_shapes=[
                pltpu.VMEM((2, D_MODEL, D_EXPERT), F32),
                pltpu.VMEM((2, D_MODEL, D_EXPERT), F32),
                pltpu.VMEM((2, D_EXPERT, D_MODEL), F32),
                pltpu.VMEM((D_MODEL, D_EXPERT), BF16),
                pltpu.VMEM((D_MODEL, D_EXPERT), BF16),
                pltpu.VMEM((D_EXPERT, D_MODEL), BF16),
                pltpu.VMEM((3, bm, D_MODEL), F32),
                pltpu.VMEM((SUBLANES, bm), I32),
                pltpu.SMEM((2, SUBLANES, bm), I32),
                pltpu.SemaphoreType.DMA((3,)),
                pltpu.SemaphoreType.DMA,
                pltpu.SemaphoreType.DMA((2,)),
            ],
        ),
        out_shape=jax.ShapeDtypeStruct((2 * n_tokens + 2 * bm, D_MODEL), F32),
        compiler_params=pltpu.CompilerParams(
            dimension_semantics=("arbitrary",),
            vmem_limit_bytes=VMEM_LIMIT),
    )(tab, xs, w_gate, w_up, w_down)


def _final_norm_body(h_ref, y0_ref, y1_ref, gfin_ref, out_ref):
    h = h_ref[...] + (y0_ref[...] + y1_ref[...])
    ms = jnp.mean(h * h, axis=-1, keepdims=True)
    out_ref[...] = h * lax.rsqrt(ms + EPS) * gfin_ref[...]


def _final_norm(h, out2, gfin):
    t = h.shape[0]
    tm = FIN_ROWS
    plane1 = t // tm
    return pl.pallas_call(
        _final_norm_body,
        name="final_norm",
        grid=(t // tm,),
        in_specs=[
            pl.BlockSpec((tm, D_MODEL), lambda i: (i, 0)),
            pl.BlockSpec((tm, D_MODEL), lambda i: (i, 0)),
            pl.BlockSpec((tm, D_MODEL), lambda i: (i + plane1, 0)),
            pl.BlockSpec((1, D_MODEL), lambda i: (0, 0)),
        ],
        out_specs=pl.BlockSpec((tm, D_MODEL), lambda i: (i, 0)),
        out_shape=jax.ShapeDtypeStruct((t, D_MODEL), F32),
        compiler_params=pltpu.CompilerParams(
            dimension_semantics=("arbitrary",),
            vmem_limit_bytes=VMEM_LIMIT),
    )(h, out2, out2, gfin)


def kernel(x, norm_mix_g, w_in, gmlp_v_norm_g, gmlp_ws, gmlp_bs, conv_w, out_norm_gmlp_g,
           out_norm_conv_g, w_out, norm_ffn_g, router_group_w, router_group_b, router_expert_w,
           router_expert_b, expert_w_gate, expert_w_up, expert_w_down, norm_final_g):
    batch, seq_len, d_model = x.shape
    t = batch * seq_len
    assert w_in.shape[0] == 1, "single-layer block"
    assert t // EXP_ROWS <= LANES, "page table holds at most LANES pages per expert"
    n_col_groups = GMLP_WIDTH // MIX_COLS
    n_pages_max = 2 * t // EXP_ROWS + N_EXPERTS
    n_page_rows = n_pages_max * EXP_ROWS
    x2 = x.reshape(t, d_model)

    vng = gmlp_v_norm_g[0].reshape(n_col_groups, 1, MIX_COLS)
    bsb = jnp.broadcast_to(gmlp_bs[0][:, :, None], (GMLP_HEADS, GMLP_BLOCK, HEAD_DIM))
    taps = conv_w[0].reshape(CONV_K, n_col_groups, MIX_COLS).transpose(1, 0, 2)
    unused = LANES - N_EXPERTS - N_GROUPS
    wr = jnp.concatenate(
        [router_expert_w[0], router_group_w[0], jnp.zeros((d_model, unused), F32)],
        axis=1).astype(BF16)
    br = jnp.concatenate(
        [router_expert_b[0], router_group_b[0], jnp.zeros((unused,), F32)])[None, :]

    ya, yb, sa, sb = _mixer_proj(x2, norm_mix_g[0][None, :], w_in[0].astype(BF16), vng,
                                 gmlp_ws[0], bsb, taps, seq_len)
    h, xs, tab = _out_route(
        ya, yb, sa, sb, x2, out_norm_gmlp_g[0][None, :], out_norm_conv_g[0][None, :],
        w_out[0].astype(BF16), norm_ffn_g[0][None, :], wr, br, n_page_rows)
    out2 = _expert_mlp(tab, xs, expert_w_gate[0], expert_w_up[0], expert_w_down[0],
                       t, n_pages_max)
    out = _final_norm(h, out2, norm_final_g[None, :])
    return out.reshape(batch, seq_len, d_model)
```

```python
import functools

import jax
import jax.numpy as jnp
from jax import lax
from jax.experimental import pallas as pl
from jax.experimental.pallas import tpu as pltpu

F32 = jnp.float32
BF16 = jnp.bfloat16
I32 = jnp.int32

D_MODEL = 2048
CHUNK = 64
GMLP_WIDTH = 1024
GMLP_HEADS = 8
HEAD_DIM = 128
GMLP_BLOCK = 128
CONV_CH = 1024
CONV_K = 3
N_GROUPS = 4
EXPERTS_PER_GROUP = 8
N_EXPERTS = 32
D_EXPERT = 512
EPS = 1e-6

LANES = 128
SUBLANES = 8

MIX_ROWS = 1024
MIX_COLS = 256
OUT_ROWS = 512
OUT_CHUNK = 512
EXP_ROWS = 256
FIN_ROWS = 512
ZERO_ROWS = 128
GROUP_LANE0 = N_EXPERTS
MASKED = -1e30
VMEM_LIMIT = 56 * 1024 * 1024

XS_COLS = D_MODEL + LANES
META_TOKEN = 0
META_GATE0 = 1
META_POS0 = 2
META_GATE1 = 3
META_POS1 = 4

TAB_ORDER = 0
TAB_EXPERT = 1
TAB_N_PAGES = 2
TAB_RUN_LEN = 3
TAB_WSLOT = 4
TAB_ROWS = 8
TAB_LANES = 256


def _rms_scale(sumsq, width):
    return lax.rsqrt(sumsq / width + EPS)


def _mixer_proj_body(seq_len, x_ref, gmix_ref, wu_ref, wv_ref, wbg_ref, wcg_ref, whv_ref,
                     vng_ref, ws_ref, bsb_ref, cw_ref,
                     ya_ref, yb_ref, sa_ref, sb_ref, xn_scr, carry_scr):
    i = pl.program_id(0)
    c = pl.program_id(1)
    tm = x_ref.shape[0]
    cw = ya_ref.shape[1]
    heads_per_step = cw // HEAD_DIM

    @pl.when(c == 0)
    def _():
        x = x_ref[...]
        ms = jnp.mean(x * x, axis=-1, keepdims=True)
        xn_scr[...] = (x * lax.rsqrt(ms + EPS) * gmix_ref[...]).astype(BF16)
        sa_ref[...] = jnp.zeros_like(sa_ref)
        sb_ref[...] = jnp.zeros_like(sb_ref)

    xn = xn_scr[...]
    project = lambda w_ref: jnp.dot(xn, w_ref[...], preferred_element_type=F32)
    u = jax.nn.gelu(project(wu_ref))
    v = jax.nn.gelu(project(wv_ref))
    bg = project(wbg_ref)
    cg = project(wcg_ref)
    hv = project(whv_ref)

    pos_i = lax.broadcasted_iota(I32, (GMLP_BLOCK, GMLP_BLOCK), 0)
    pos_j = lax.broadcasted_iota(I32, (GMLP_BLOCK, GMLP_BLOCK), 1)
    chunk_shift = CHUNK.bit_length() - 1
    causal = (pos_i >> chunk_shift) >= (pos_j >> chunk_shift)
    vng = vng_ref[0]
    ya_heads = []
    for j in range(heads_per_step):
        head = c * heads_per_step + j
        lanes = slice(j * HEAD_DIM, (j + 1) * HEAD_DIM)
        vj = v[:, lanes]
        ms = jnp.mean(vj * vj, axis=-1, keepdims=True)
        vn = (vj * lax.rsqrt(ms + EPS) * vng[:, lanes]).astype(BF16)
        w_mix = jnp.where(causal, ws_ref[head], 0.0).astype(BF16)
        bias = bsb_ref[head]
        mixed = [
            jnp.dot(w_mix, vn[p * GMLP_BLOCK:(p + 1) * GMLP_BLOCK, :],
                    preferred_element_type=F32) + bias
            for p in range(tm // GMLP_BLOCK)
        ]
        ya_heads.append(u[:, lanes] * jnp.concatenate(mixed, axis=0))
    ya = jnp.concatenate(ya_heads, axis=1) if heads_per_step > 1 else ya_heads[0]

    z = cg * hv
    prev = carry_scr[c]
    seq_start = (i * tm) % seq_len == 0
    prev = jnp.where(seq_start, 0.0, prev)
    row = lax.broadcasted_iota(I32, (tm, cw), 0)
    z1 = jnp.where(row == 0, prev[SUBLANES - 1:SUBLANES, :], pltpu.roll(z, 1, axis=0))
    z2 = jnp.where(row == 0, prev[SUBLANES - 2:SUBLANES - 1, :],
                   jnp.where(row == 1, prev[SUBLANES - 1:SUBLANES, :],
                             pltpu.roll(z, 2, axis=0)))
    taps = cw_ref[0]
    conv = taps[0:1, :] * z2 + taps[1:2, :] * z1 + taps[2:3, :] * z
    yb = bg * conv
    carry_scr[c] = z[tm - SUBLANES:tm, :]

    ya_ref[...] = ya.astype(BF16)
    yb_ref[...] = yb.astype(BF16)
    ya2 = ya * ya
    yb2 = yb * yb
    sa_ref[...] += sum(ya2[:, k * LANES:(k + 1) * LANES] for k in range(cw // LANES))
    sb_ref[...] += sum(yb2[:, k * LANES:(k + 1) * LANES] for k in range(cw // LANES))


def _mixer_proj(x2, gmix, w_in, vng, ws, bsb, conv_taps, seq_len):
    t = x2.shape[0]
    cw = MIX_COLS
    tm = MIX_ROWS
    n_groups = GMLP_WIDTH // cw
    grid = (t // tm, n_groups)
    w_part = lambda k: pl.BlockSpec((D_MODEL, cw), lambda i, c: (0, k * n_groups + c))
    return pl.pallas_call(
        functools.partial(_mixer_proj_body, seq_len),
        name="mixer_proj",
        grid=grid,
        in_specs=[
            pl.BlockSpec((tm, D_MODEL), lambda i, c: (i, 0)),
            pl.BlockSpec((1, D_MODEL), lambda i, c: (0, 0)),
            w_part(0), w_part(1), w_part(2), w_part(3), w_part(4),
            pl.BlockSpec((1, 1, cw), lambda i, c: (c, 0, 0)),
            pl.BlockSpec((GMLP_HEADS, GMLP_BLOCK, GMLP_BLOCK), lambda i, c: (0, 0, 0)),
            pl.BlockSpec((GMLP_HEADS, GMLP_BLOCK, HEAD_DIM), lambda i, c: (0, 0, 0)),
            pl.BlockSpec((1, CONV_K, cw), lambda i, c: (c, 0, 0)),
        ],
        out_specs=[
            pl.BlockSpec((tm, cw), lambda i, c: (i, c)),
            pl.BlockSpec((tm, cw), lambda i, c: (i, c)),
            pl.BlockSpec((tm, LANES), lambda i, c: (i, 0)),
            pl.BlockSpec((tm, LANES), lambda i, c: (i, 0)),
        ],
        out_shape=[
            jax.ShapeDtypeStruct((t, GMLP_WIDTH), BF16),
            jax.ShapeDtypeStruct((t, CONV_CH), BF16),
            jax.ShapeDtypeStruct((t, LANES), F32),
            jax.ShapeDtypeStruct((t, LANES), F32),
        ],
        scratch_shapes=[
            pltpu.VMEM((tm, D_MODEL), BF16),
            pltpu.VMEM((n_groups, SUBLANES, cw), F32),
        ],
        compiler_params=pltpu.CompilerParams(
            dimension_semantics=("arbitrary", "arbitrary"),
            vmem_limit_bytes=VMEM_LIMIT),
    )(x2, gmix, w_in, w_in, w_in, w_in, w_in, vng, ws, bsb, conv_taps)


def _lane_cumsum(v, lane):
    shift = 1
    while shift < N_EXPERTS:
        v = v + jnp.where(lane >= shift, pltpu.roll(v, shift, axis=1), 0.0)
        shift *= 2
    return v


def _lane_dense(col_values, lane):
    tile = jnp.zeros(lane.shape, F32)
    for k, col in enumerate(col_values):
        tile = jnp.where(lane == k, col, tile)
    return tile.T[0:SUBLANES, :]


def _finish_pages(n_page_rows, cnt_ref, ptab_ref, xs_ref, tab_ref, zero_scr, sem):
    page_shift = EXP_ROWS.bit_length() - 1
    n_slots = tab_ref.shape[1]
    zero_scr[...] = jnp.zeros_like(zero_scr)

    def clear(s, carry):
        for row in range(TAB_ROWS):
            tab_ref[row, s] = 0
        return carry

    lax.fori_loop(0, n_slots, clear, 0)

    def pages_of(e):
        return (cnt_ref[0, e] + (EXP_ROWS - 1)) >> page_shift

    def list_pages(e, carry):
        first_slot, run = carry

        def put(j, carry):
            tab_ref[TAB_ORDER, first_slot + j] = ptab_ref[e, j]
            tab_ref[TAB_EXPERT, first_slot + j] = e
            tab_ref[TAB_WSLOT, first_slot + j] = run & 1
            return carry

        n_pages_e = pages_of(e)
        lax.fori_loop(0, n_pages_e, put, 0)

        @pl.when(n_pages_e > 0)
        def _():
            tab_ref[TAB_RUN_LEN, first_slot] = n_pages_e

        return first_slot + n_pages_e, run + jnp.where(n_pages_e > 0, 1, 0)

    n_pages, _ = lax.fori_loop(0, N_EXPERTS, list_pages, (0, 0))
    tab_ref[TAB_N_PAGES, 0] = n_pages
    last_page = tab_ref[TAB_ORDER, n_pages - 1]
    last_expert = tab_ref[TAB_EXPERT, n_pages - 1]

    def repeat_last(s, carry):
        tab_ref[TAB_ORDER, s] = last_page
        tab_ref[TAB_EXPERT, s] = last_expert
        return carry

    lax.fori_loop(n_pages, n_slots, repeat_last, 0)

    def pad_copies(e):
        used = cnt_ref[0, e] & (EXP_ROWS - 1)
        last = ptab_ref[e, jnp.maximum(pages_of(e) - 1, 0)]
        return _zero_range_copies(zero_scr, xs_ref, sem, last * EXP_ROWS + used,
                                  (EXP_ROWS - used) & (EXP_ROWS - 1))

    def tail_copies(p):
        return [
            pltpu.make_async_copy(
                zero_scr,
                xs_ref.at[pl.ds(pl.multiple_of(p * EXP_ROWS + part * ZERO_ROWS, ZERO_ROWS),
                                ZERO_ROWS)],
                sem)
            for part in range(EXP_ROWS // ZERO_ROWS)
        ]

    def start_pads(e, carry):
        for needed, copy in pad_copies(e):
            pl.when(needed)(copy.start)
        return carry

    def wait_pads(e, carry):
        for needed, copy in pad_copies(e):
            pl.when(needed)(copy.wait)
        return carry

    def start_tail(p, carry):
        for copy in tail_copies(p):
            copy.start()
        return carry

    def wait_tail(p, carry):
        for copy in tail_copies(p):
            copy.wait()
        return carry

    total_pages = n_page_rows // EXP_ROWS
    lax.fori_loop(0, N_EXPERTS, start_pads, 0)
    lax.fori_loop(n_pages, total_pages, start_tail, 0)
    lax.fori_loop(0, N_EXPERTS, wait_pads, 0)
    lax.fori_loop(n_pages, total_pages, wait_tail, 0)


def _out_route_body(trash_row0, ya_ref, yb_ref, sa_ref, sb_ref, x_ref, ga_ref, gb_ref, wout_ref,
                    gffn_ref, wr_ref, br_ref, h_ref, xs_ref, tab_ref,
                    row_buf, pos_vmem, pos_smem, st_scr, ptab_scr, cnt_vmem, ptab_vmem,
                    cnt_smem, ptab_smem, zero_scr, row_sem, pos_sem, fill_sem):
    i = pl.program_id(0)
    n_steps = pl.num_programs(0) - 1
    tm = x_ref.shape[0]
    slot = lax.rem(i, 2)
    prev = 1 - slot

    def row_copy(r, k, prev=prev):
        return pltpu.make_async_copy(
            row_buf.at[prev, pl.ds(r, 1)], xs_ref.at[pl.ds(pos_smem[prev, k, r], 1)], row_sem)

    def wait_step_rows():
        for _ in range(2):
            pltpu.make_async_copy(row_buf.at[0], xs_ref.at[pl.ds(0, tm)], row_sem).wait()

    def pos_copy(s):
        return pltpu.make_async_copy(pos_vmem, pos_smem.at[s], pos_sem)

    @pl.when(i == 0)
    def _():
        st_scr[...] = jnp.zeros_like(st_scr)
        ptab_scr[...] = jnp.zeros_like(ptab_scr)
        row_buf[1] = jnp.zeros((tm, XS_COLS), F32)
        which = lax.broadcasted_iota(I32, (SUBLANES, tm), 0)
        pos_vmem[...] = (trash_row0 + lax.broadcasted_iota(I32, (SUBLANES, tm), 1)
                         + jnp.where(which == 1, tm, 0))
        pos_copy(1).start()

    @pl.when(i > 0)
    def _():
        wait_step_rows()

    def route_block(slot):
        prev = 1 - slot
        rows_per_chunk = tm // (D_MODEL // OUT_CHUNK)

        ra = _rms_scale(jnp.sum(sa_ref[...], axis=-1, keepdims=True), GMLP_WIDTH)
        rb = _rms_scale(jnp.sum(sb_ref[...], axis=-1, keepdims=True), CONV_CH)
        yna = (ya_ref[...].astype(F32) * ra * ga_ref[...]).astype(BF16)
        ynb = (yb_ref[...].astype(F32) * rb * gb_ref[...]).astype(BF16)
        yn = jnp.concatenate([yna, ynb], axis=1)
        for j in range(D_MODEL // OUT_CHUNK):
            cols = slice(j * OUT_CHUNK, (j + 1) * OUT_CHUNK)
            h_ref[:, cols] = x_ref[:, cols] + jnp.dot(
                yn, wout_ref[:, cols], preferred_element_type=F32)
            if j == 0:
                pos_copy(prev).wait()
            for r in range(j * rows_per_chunk, (j + 1) * rows_per_chunk):
                for k in range(2):
                    row_copy(r, k, prev).start(priority=k)
        h = h_ref[...]

        ms = jnp.mean(h * h, axis=-1, keepdims=True)
        hn = h * lax.rsqrt(ms + EPS) * gffn_ref[...]
        logits = jnp.dot(hn.astype(BF16), wr_ref[...], preferred_element_type=F32) + br_ref[...]

        lane = lax.broadcasted_iota(I32, (tm, LANES), 1)
        lane_f = lane.astype(F32)
        big = float(LANES)

        is_g = (lane >= GROUP_LANE0) & (lane < GROUP_LANE0 + N_GROUPS)
        lg = jnp.where(is_g, logits, MASKED)
        lg_top = jnp.max(lg, axis=-1, keepdims=True)
        g_idx = jnp.min(jnp.where(is_g & (lg == lg_top), lane_f - GROUP_LANE0, big),
                        axis=-1, keepdims=True).astype(I32)
        pg_top = 1.0 / jnp.sum(jnp.where(is_g, jnp.exp(lg - lg_top), 0.0), axis=-1, keepdims=True)

        group_shift = EXPERTS_PER_GROUP.bit_length() - 1
        in_grp = (lane < N_EXPERTS) & ((lane >> group_shift) == g_idx)
        le = jnp.where(in_grp, logits, MASKED)
        le1 = jnp.max(le, axis=-1, keepdims=True)
        e1 = jnp.min(jnp.where(in_grp & (le == le1), lane_f, big), axis=-1, keepdims=True)
        denom = jnp.sum(jnp.where(in_grp, jnp.exp(le - le1), 0.0), axis=-1, keepdims=True)
        rest = in_grp & (lane_f != e1)
        le2 = jnp.max(jnp.where(rest, le, MASKED), axis=-1, keepdims=True)
        e2 = jnp.min(jnp.where(rest & (le == le2), lane_f, big), axis=-1, keepdims=True)
        q1 = 1.0 / denom
        q2 = jnp.exp(le2 - le1) / denom
        qs = q1 + q2
        gate1 = pg_top * (q1 / qs)
        gate2 = pg_top * (q2 / qs)

        is1 = lane_f == e1
        is2 = lane_f == e2
        sel = jnp.where(is1 | is2, 1.0, 0.0)
        t_i = lax.broadcasted_iota(I32, (tm, tm), 0)
        t_j = lax.broadcasted_iota(I32, (tm, tm), 1)
        before = jnp.where(t_j < t_i, 1.0, 0.0).astype(BF16)
        run0 = st_scr[0:1, :]
        cur_page = st_scr[1:2, :]
        next_free = st_scr[2:3, :]
        cum = jnp.dot(before, sel.astype(BF16), preferred_element_type=F32) + run0

        lane_row = lax.broadcasted_iota(I32, (1, LANES), 1)
        run1 = run0 + jnp.sum(sel, axis=0, keepdims=True)
        pages0 = jnp.ceil(run0 / EXP_ROWS)
        n_new = jnp.ceil(run1 / EXP_ROWS) - pages0
        first_new = next_free + _lane_cumsum(n_new, lane_row) - n_new

        def pick(mask, per_expert):
            return jnp.sum(jnp.where(mask, per_expert, 0.0), axis=-1, keepdims=True)

        def place(mask):
            rank = pick(mask, cum)
            page_idx = jnp.floor(rank / EXP_ROWS)
            owned = pick(mask, pages0)
            page = jnp.where(page_idx < owned, pick(mask, cur_page),
                             pick(mask, first_new) + (page_idx - owned))
            return page * EXP_ROWS + (rank - page_idx * EXP_ROWS)

        pos1 = place(is1)
        pos2 = place(is2)

        st_scr[0:1, :] = run1
        st_scr[1:2, :] = jnp.where(n_new > 0, first_new + n_new - 1, cur_page)
        st_scr[2:3, :] = next_free + jnp.sum(n_new, axis=-1, keepdims=True)

        per_expert = jnp.concatenate(
            [pages0, n_new, first_new, jnp.zeros((LANES - 3, LANES), F32)], axis=0).T
        owned_c = per_expert[0:N_EXPERTS, 0:1]
        n_new_c = per_expert[0:N_EXPERTS, 1:2]
        first_c = per_expert[0:N_EXPERTS, 2:3]
        idx = lax.broadcasted_iota(I32, (N_EXPERTS, LANES), 1).astype(F32)
        fresh = (idx >= owned_c) & (idx < owned_c + n_new_c)
        ptab_scr[...] = jnp.where(fresh, first_c + (idx - owned_c), ptab_scr[...])
        cnt_vmem[...] = jnp.broadcast_to(run1, cnt_vmem.shape).astype(I32)
        ptab_vmem[...] = ptab_scr[...].astype(I32)

        token = (i * tm + lax.broadcasted_iota(I32, (tm, 1), 0) + 1).astype(F32)
        meta = jnp.zeros((tm, LANES), F32)
        for k, col in ((META_TOKEN, token), (META_GATE0, gate1), (META_POS0, pos1),
                       (META_GATE1, gate2), (META_POS1, pos2)):
            meta = jnp.where(lane == k, col, meta)
        row_buf[slot, :, 0:D_MODEL] = hn
        row_buf[slot, :, D_MODEL:XS_COLS] = meta
        pos_vmem[...] = _lane_dense([pos1, pos2], lane).astype(I32)
        pos_copy(slot).start()

    for static_slot in range(2):
        pl.when((i < n_steps) & (slot == static_slot))(
            functools.partial(route_block, static_slot))

    @pl.when(i == n_steps)
    def _():
        def issue(r, carry):
            for k in range(2):
                row_copy(r, k).start(priority=k)
            return carry

        pos_copy(prev).wait()
        lax.fori_loop(0, tm, issue, 0, unroll=8)
        wait_step_rows()

        for vec, scal in ((cnt_vmem, cnt_smem), (ptab_vmem, ptab_smem)):
            copy = pltpu.make_async_copy(vec, scal, fill_sem)
            copy.start()
            copy.wait()
        _finish_pages(trash_row0, cnt_smem, ptab_smem, xs_ref, tab_ref, zero_scr, fill_sem)


def _out_route(ya, yb, sa, sb, x2, ga, gb, w_out, gffn, wr, br, n_page_rows):
    t = x2.shape[0]
    tm = OUT_ROWS
    n_steps = t // tm
    const = lambda i: (0, 0)
    rows = lambda i: (jnp.minimum(i, n_steps - 1), 0)
    return pl.pallas_call(
        functools.partial(_out_route_body, n_page_rows),
        name="out_route",
        grid=(n_steps + 1,),
        in_specs=[
            pl.BlockSpec((tm, GMLP_WIDTH), rows),
            pl.BlockSpec((tm, CONV_CH), rows),
            pl.BlockSpec((tm, LANES), rows),
            pl.BlockSpec((tm, LANES), rows),
            pl.BlockSpec((tm, D_MODEL), rows),
            pl.BlockSpec((1, GMLP_WIDTH), const),
            pl.BlockSpec((1, CONV_CH), const),
            pl.BlockSpec((D_MODEL, D_MODEL), const, pipeline_mode=pl.Buffered(1)),
            pl.BlockSpec((1, D_MODEL), const),
            pl.BlockSpec((D_MODEL, LANES), const),
            pl.BlockSpec((1, LANES), const),
        ],
        out_specs=[
            pl.BlockSpec((tm, D_MODEL), rows),
            pl.BlockSpec(memory_space=pl.ANY),
            pl.BlockSpec(memory_space=pltpu.SMEM),
        ],
        out_shape=[
            jax.ShapeDtypeStruct((t, D_MODEL), F32),
            jax.ShapeDtypeStruct((n_page_rows + 2 * tm, XS_COLS), F32),
            jax.ShapeDtypeStruct((TAB_ROWS, TAB_LANES), I32),
        ],
        scratch_shapes=[
            pltpu.VMEM((2, tm, XS_COLS), F32),
            pltpu.VMEM((SUBLANES, tm), I32),
            pltpu.SMEM((2, SUBLANES, tm), I32),
            pltpu.VMEM((SUBLANES, LANES), F32),
            pltpu.VMEM((N_EXPERTS, LANES), F32),
            pltpu.VMEM((SUBLANES, LANES), I32),
            pltpu.VMEM((N_EXPERTS, LANES), I32),
            pltpu.SMEM((SUBLANES, LANES), I32),
            pltpu.SMEM((N_EXPERTS, LANES), I32),
            pltpu.VMEM((ZERO_ROWS, XS_COLS), F32),
            pltpu.SemaphoreType.DMA,
            pltpu.SemaphoreType.DMA,
            pltpu.SemaphoreType.DMA,
        ],
        compiler_params=pltpu.CompilerParams(
            dimension_semantics=("arbitrary",),
            vmem_limit_bytes=VMEM_LIMIT),
    )(ya, yb, sa, sb, x2, ga, gb, w_out, gffn, wr, br)


def _zero_range_copies(zero_scr, xs_ref, sem, start, length):
    head = length & (SUBLANES - 1)
    copies = []
    for j in range(SUBLANES - 1):
        copy = pltpu.make_async_copy(
            zero_scr.at[pl.ds(0, 1)], xs_ref.at[pl.ds(start + j, 1)], sem)
        copies.append((j < head, copy))
    body_start = start + head
    body = length - head
    size = ZERO_ROWS
    while size >= SUBLANES:
        offset = pl.multiple_of(body_start + (body & ~(2 * size - 1)), SUBLANES)
        copy = pltpu.make_async_copy(
            zero_scr.at[pl.ds(0, size)], xs_ref.at[pl.ds(offset, size)], sem)
        copies.append(((body & size) != 0, copy))
        size //= 2
    return copies


def _expert_mlp_body(n_tokens, tab_ref, xs_ref, wg_hbm, wu_hbm, wd_hbm, out2_ref,
                     wg_f32, wu_f32, wd_f32, wg_scr, wu_scr, wd_scr, out_buf, dest_vmem,
                     dest_smem, row_sem, dest_sem, w_sem):
    s = pl.program_id(0)
    n_pages = tab_ref[TAB_N_PAGES, 0]
    bm = xs_ref.shape[0]
    cur = lax.rem(s, 3)
    prv = lax.rem(s + 2, 3)
    par = lax.rem(s, 2)
    ppar = 1 - par
    trash_row0 = 2 * n_tokens
    expert = tab_ref[TAB_EXPERT, s]
    run_len = tab_ref[TAB_RUN_LEN, s]
    wslot = tab_ref[TAB_WSLOT, s]

    def weight_copies(e, slot):
        return [
            pltpu.make_async_copy(src.at[e], dst.at[slot], w_sem.at[slot])
            for src, dst in ((wg_hbm, wg_f32), (wu_hbm, wu_f32), (wd_hbm, wd_f32))
        ]

    def row_copy(r, prv=prv):
        return pltpu.make_async_copy(
            out_buf.at[prv, pl.ds(r, 1)], out2_ref.at[pl.ds(dest_smem[ppar, 0, r], 1)],
            row_sem.at[prv])

    def wait_rows(slot):
        pltpu.make_async_copy(out_buf.at[0], out2_ref.at[pl.ds(0, bm)], row_sem.at[slot]).wait()

    def dest_copy(p):
        return pltpu.make_async_copy(dest_vmem, dest_smem.at[p], dest_sem)

    @pl.when(s == 0)
    def _():
        out_buf[2] = jnp.zeros((bm, D_MODEL), F32)
        for half in range(2):
            clear = pltpu.make_async_copy(
                out_buf.at[2], out2_ref.at[pl.ds(trash_row0 + half * bm, bm)], dest_sem)
            clear.start()
            clear.wait()

        def fill(r, carry):
            dest_smem[1, 0, r] = trash_row0 + bm + r
            return carry

        lax.fori_loop(0, bm, fill, 0)
        for copy in weight_copies(expert, wslot):
            copy.start()

    @pl.when((s >= 2) & (s <= n_pages))
    def _():
        wait_rows(cur)

    @pl.when((s >= 1) & (s <= n_pages))
    def _():
        dest_copy(ppar).wait()

    @pl.when((s < n_pages) & (run_len > 0))
    def _():
        for copy in weight_copies(expert, wslot):
            copy.wait()
        wg_scr[...] = wg_f32[wslot].astype(BF16)
        wu_scr[...] = wu_f32[wslot].astype(BF16)
        wd_scr[...] = wd_f32[wslot].astype(BF16)
        next_run = s + run_len

        @pl.when(next_run < n_pages)
        def _():
            for copy in weight_copies(tab_ref[TAB_EXPERT, next_run], 1 - wslot):
                copy.start()

    def compute_page(cur):
        prv = (cur + 2) % 3
        down_parts = 2
        n_groups = 2 + down_parts
        group_rows = bm // n_groups
        part_d = D_MODEL // down_parts

        def scatter_group(g):
            for r in range(g * group_rows, (g + 1) * group_rows):
                row_copy(r, prv).start(priority=r % 2)

        x = xs_ref[:, 0:D_MODEL].astype(BF16)
        gate = jnp.dot(x, wg_scr[...], preferred_element_type=F32)

        meta = xs_ref[:, D_MODEL:XS_COLS]
        own_row = (tab_ref[TAB_ORDER, s] * bm
                   + lax.broadcasted_iota(I32, (bm, 1), 0)).astype(F32)
        local = lax.broadcasted_iota(I32, (bm, 1), 0).astype(F32)
        token = meta[:, META_TOKEN:META_TOKEN + 1]
        first = meta[:, META_POS0:META_POS0 + 1] == own_row
        weight = jnp.where(first, meta[:, META_GATE0:META_GATE0 + 1],
                           meta[:, META_GATE1:META_GATE1 + 1])
        plane = jnp.where(first, 0.0, float(n_tokens))
        dest = jnp.where(token > 0.0, plane + token - 1.0,
                         (trash_row0 + par * bm).astype(F32) + local)
        lane = lax.broadcasted_iota(I32, (bm, LANES), 1)
        dest_vmem[...] = _lane_dense([dest], lane).astype(I32)
        dest_copy(par).start()
        scatter_group(0)
        up = jnp.dot(x, wu_scr[...], preferred_element_type=F32)
        scatter_group(1)
        hidden = (jax.nn.silu(gate) * up).astype(BF16)

        for part in range(down_parts):
            cols = slice(part * part_d, (part + 1) * part_d)
            out_buf[cur, :, cols] = jnp.dot(
                hidden, wd_scr[:, cols], preferred_element_type=F32) * weight
            scatter_group(2 + part)

    for static_cur in range(3):
        pl.when((s < n_pages) & (cur == static_cur))(
            functools.partial(compute_page, static_cur))

    @pl.when(s == n_pages)
    def _():
        def issue(pair, carry):
            for k in range(2):
                row_copy(2 * pair + k).start(priority=k)
            return carry

        lax.fori_loop(0, bm // 2, issue, 0, unroll=4)
        wait_rows(lax.rem(s + 1, 3))
        wait_rows(prv)


def _expert_mlp(tab, xs, w_gate, w_up, w_down, n_tokens, n_pages_max):
    bm = EXP_ROWS
    page_of = lambda s, tab: (tab[TAB_ORDER, s], 0)
    return pl.pallas_call(
        functools.partial(_expert_mlp_body, n_tokens),
        name="expert_mlp",
        grid_spec=pltpu.PrefetchScalarGridSpec(
            num_scalar_prefetch=1,
            grid=(n_pages_max + 1,),
            in_specs=[
                pl.BlockSpec((bm, XS_COLS), page_of),
                pl.BlockSpec(memory_space=pl.ANY),
                pl.BlockSpec(memory_space=pl.ANY),
                pl.BlockSpec(memory_space=pl.ANY),
            ],
            out_specs=pl.BlockSpec(memory_space=pl.ANY),
            scratch_shapes=[
                pltpu.VMEM((2, D_MODEL, D_EXPERT), F32),
                pltpu.VMEM((2, D_MODEL, D_EXPERT), F32),
                pltpu.VMEM((2, D_EXPERT, D_MODEL), F32),
                pltpu.VMEM((D_MODEL, D_EXPERT), BF16),
                pltpu.VMEM((D_MODEL, D_EXPERT), BF16),
                pltpu.VMEM((D_EXPERT, D_MODEL), BF16),
                pltpu.VMEM((3, bm, D_MODEL), F32),
                pltpu.VMEM((SUBLANES, bm), I32),
                pltpu.SMEM((2, SUBLANES, bm), I32),
                pltpu.SemaphoreType.DMA((3,)),
                pltpu.SemaphoreType.DMA,
                pltpu.SemaphoreType.DMA((2,)),
            ],
        ),
        out_shape=jax.ShapeDtypeStruct((2 * n_tokens + 2 * bm, D_MODEL), F32),
        compiler_params=pltpu.CompilerParams(
            dimension_semantics=("arbitrary",),
            vmem_limit_bytes=VMEM_LIMIT),
    )(tab, xs, w_gate, w_up, w_down)


def _final_norm_body(h_ref, y0_ref, y1_ref, gfin_ref, out_ref):
    h = h_ref[...] + (y0_ref[...] + y1_ref[...])
    ms = jnp.mean(h * h, axis=-1, keepdims=True)
    out_ref[...] = h * lax.rsqrt(ms + EPS) * gfin_ref[...]


def _final_norm(h, out2, gfin):
    t = h.shape[0]
    tm = FIN_ROWS
    plane1 = t // tm
    return pl.pallas_call(
        _final_norm_body,
        name="final_norm",
        grid=(t // tm,),
        in_specs=[
            pl.BlockSpec((tm, D_MODEL), lambda i: (i, 0)),
            pl.BlockSpec((tm, D_MODEL), lambda i: (i, 0)),
            pl.BlockSpec((tm, D_MODEL), lambda i: (i + plane1, 0)),
            pl.BlockSpec((1, D_MODEL), lambda i: (0, 0)),
        ],
        out_specs=pl.BlockSpec((tm, D_MODEL), lambda i: (i, 0)),
        out_shape=jax.ShapeDtypeStruct((t, D_MODEL), F32),
        compiler_params=pltpu.CompilerParams(
            dimension_semantics=("arbitrary",),
            vmem_limit_bytes=VMEM_LIMIT),
    )(h, out2, out2, gfin)


def kernel(x, norm_mix_g, w_in, gmlp_v_norm_g, gmlp_ws, gmlp_bs, conv_w, out_norm_gmlp_g,
           out_norm_conv_g, w_out, norm_ffn_g, router_group_w, router_group_b, router_expert_w,
           router_expert_b, expert_w_gate, expert_w_up, expert_w_down, norm_final_g):
    batch, seq_len, d_model = x.shape
    t = batch * seq_len
    assert w_in.shape[0] == 1, "single-layer block"
    assert t // EXP_ROWS <= LANES, "page table holds at most LANES pages per expert"
    n_col_groups = GMLP_WIDTH // MIX_COLS
    n_pages_max = 2 * t // EXP_ROWS + N_EXPERTS
    n_page_rows = n_pages_max * EXP_ROWS
    x2 = x.reshape(t, d_model)

    vng = gmlp_v_norm_g[0].reshape(n_col_groups, 1, MIX_COLS)
    bsb = jnp.broadcast_to(gmlp_bs[0][:, :, None], (GMLP_HEADS, GMLP_BLOCK, HEAD_DIM))
    taps = conv_w[0].reshape(CONV_K, n_col_groups, MIX_COLS).transpose(1, 0, 2)
    unused = LANES - N_EXPERTS - N_GROUPS
    wr = jnp.concatenate(
        [router_expert_w[0], router_group_w[0], jnp.zeros((d_model, unused), F32)],
        axis=1).astype(BF16)
    br = jnp.concatenate(
        [router_expert_b[0], router_group_b[0], jnp.zeros((unused,), F32)])[None, :]

    ya, yb, sa, sb = _mixer_proj(x2, norm_mix_g[0][None, :], w_in[0].astype(BF16), vng,
                                 gmlp_ws[0], bsb, taps, seq_len)
    h, xs, tab = _out_route(
        ya, yb, sa, sb, x2, out_norm_gmlp_g[0][None, :], out_norm_conv_g[0][None, :],
        w_out[0].astype(BF16), norm_ffn_g[0][None, :], wr, br, n_page_rows)
    out2 = _expert_mlp(tab, xs, expert_w_gate[0], expert_w_up[0], expert_w_down[0],
                       t, n_pages_max)
    out = _final_norm(h, out2, norm_final_g[None, :])
    return out.reshape(batch, seq_len, d_model)
```

```python
import functools

import jax
import jax.numpy as jnp
from jax import lax
from jax.experimental import pallas as pl
from jax.experimental.pallas import tpu as pltpu

F32 = jnp.float32
BF16 = jnp.bfloat16
I32 = jnp.int32

D_MODEL = 2048
CHUNK = 64
GMLP_WIDTH = 1024
GMLP_HEADS = 8
HEAD_DIM = 128
GMLP_BLOCK = 128
CONV_CH = 1024
CONV_K = 3
N_GROUPS = 4
EXPERTS_PER_GROUP = 8
N_EXPERTS = 32
D_EXPERT = 512
EPS = 1e-6

LANES = 128
SUBLANES = 8

MIX_ROWS = 1024
MIX_COLS = 256
OUT_ROWS = 512
OUT_CHUNK = 512
EXP_ROWS = 256
FIN_ROWS = 512
ZERO_ROWS = 128
GROUP_LANE0 = N_EXPERTS
MASKED = -1e30
VMEM_LIMIT = 56 * 1024 * 1024

XS_COLS = D_MODEL + LANES
META_TOKEN = 0
META_GATE0 = 1
META_POS0 = 2
META_GATE1 = 3
META_POS1 = 4

TAB_ORDER = 0
TAB_EXPERT = 1
TAB_N_PAGES = 2
TAB_RUN_LEN = 3
TAB_WSLOT = 4
TAB_ROWS = 8
TAB_LANES = 256


def _rms_scale(sumsq, width):
    return lax.rsqrt(sumsq / width + EPS)


def _mixer_proj_body(seq_len, x0_ref, x1_ref, x2_ref, x3_ref, gmix_ref, wu_ref, wv_ref, wbg_ref,
                     wcg_ref, whv_ref, vng_ref, ws_ref, bsb_ref, cw_ref,
                     ya_ref, yb_ref, sa_ref, sb_ref, xn_scr, carry_scr):
    i = pl.program_id(0)
    c = pl.program_id(1)
    tm = x0_ref.shape[0]
    cw = ya_ref.shape[1]
    heads_per_step = cw // HEAD_DIM

    @pl.when(c == 0)
    def _():
        x = jnp.concatenate([x0_ref[...], x1_ref[...], x2_ref[...], x3_ref[...]], axis=1)
        ms = jnp.mean(x * x, axis=-1, keepdims=True)
        xn_scr[...] = (x * lax.rsqrt(ms + EPS) * gmix_ref[...]).astype(BF16)
        sa_ref[...] = jnp.zeros_like(sa_ref)
        sb_ref[...] = jnp.zeros_like(sb_ref)

    xn = xn_scr[...]
    project = lambda w_ref: jnp.dot(xn, w_ref[...], preferred_element_type=F32)
    u = jax.nn.gelu(project(wu_ref))
    v = jax.nn.gelu(project(wv_ref))
    bg = project(wbg_ref)
    cg = project(wcg_ref)
    hv = project(whv_ref)

    pos_i = lax.broadcasted_iota(I32, (GMLP_BLOCK, GMLP_BLOCK), 0)
    pos_j = lax.broadcasted_iota(I32, (GMLP_BLOCK, GMLP_BLOCK), 1)
    chunk_shift = CHUNK.bit_length() - 1
    causal = (pos_i >> chunk_shift) >= (pos_j >> chunk_shift)
    vng = vng_ref[0]
    ya_heads = []
    for j in range(heads_per_step):
        head = c * heads_per_step + j
        lanes = slice(j * HEAD_DIM, (j + 1) * HEAD_DIM)
        vj = v[:, lanes]
        ms = jnp.mean(vj * vj, axis=-1, keepdims=True)
        vn = (vj * lax.rsqrt(ms + EPS) * vng[:, lanes]).astype(BF16)
        w_mix = jnp.where(causal, ws_ref[head], 0.0).astype(BF16)
        bias = bsb_ref[head]
        mixed = [
            jnp.dot(w_mix, vn[p * GMLP_BLOCK:(p + 1) * GMLP_BLOCK, :],
                    preferred_element_type=F32) + bias
            for p in range(tm // GMLP_BLOCK)
        ]
        ya_heads.append(u[:, lanes] * jnp.concatenate(mixed, axis=0))
    ya = jnp.concatenate(ya_heads, axis=1) if heads_per_step > 1 else ya_heads[0]

    z = cg * hv
    prev = carry_scr[c]
    seq_start = (i * tm) % seq_len == 0
    prev = jnp.where(seq_start, 0.0, prev)
    row = lax.broadcasted_iota(I32, (tm, cw), 0)
    z1 = jnp.where(row == 0, prev[SUBLANES - 1:SUBLANES, :], pltpu.roll(z, 1, axis=0))
    z2 = jnp.where(row == 0, prev[SUBLANES - 2:SUBLANES - 1, :],
                   jnp.where(row == 1, prev[SUBLANES - 1:SUBLANES, :],
                             pltpu.roll(z, 2, axis=0)))
    taps = cw_ref[0]
    conv = taps[0:1, :] * z2 + taps[1:2, :] * z1 + taps[2:3, :] * z
    yb = bg * conv
    carry_scr[c] = z[tm - SUBLANES:tm, :]

    ya_ref[...] = ya.astype(BF16)
    yb_ref[...] = yb.astype(BF16)
    ya2 = ya * ya
    yb2 = yb * yb
    sa_ref[...] += sum(ya2[:, k * LANES:(k + 1) * LANES] for k in range(cw // LANES))
    sb_ref[...] += sum(yb2[:, k * LANES:(k + 1) * LANES] for k in range(cw // LANES))


def _mixer_proj(x2, gmix, w_in, vng, ws, bsb, conv_taps, seq_len):
    t = x2.shape[0]
    cw = MIX_COLS
    tm = MIX_ROWS
    n_groups = GMLP_WIDTH // cw
    grid = (t // tm, n_groups)
    w_part = lambda k: pl.BlockSpec((D_MODEL, cw), lambda i, c: (0, k * n_groups + c))
    assert n_groups == 4
    last_block = t // tm - 1
    x_part = lambda q: pl.BlockSpec(
        (tm, D_MODEL // 4),
        lambda i, c: (jnp.minimum(i + jnp.where(c > q, 1, 0), last_block), q))
    return pl.pallas_call(
        functools.partial(_mixer_proj_body, seq_len),
        name="mixer_proj",
        grid=grid,
        in_specs=[
            x_part(0), x_part(1), x_part(2), x_part(3),
            pl.BlockSpec((1, D_MODEL), lambda i, c: (0, 0)),
            w_part(0), w_part(1), w_part(2), w_part(3), w_part(4),
            pl.BlockSpec((1, 1, cw), lambda i, c: (c, 0, 0)),
            pl.BlockSpec((GMLP_HEADS, GMLP_BLOCK, GMLP_BLOCK), lambda i, c: (0, 0, 0)),
            pl.BlockSpec((GMLP_HEADS, GMLP_BLOCK, HEAD_DIM), lambda i, c: (0, 0, 0)),
            pl.BlockSpec((1, CONV_K, cw), lambda i, c: (c, 0, 0)),
        ],
        out_specs=[
            pl.BlockSpec((tm, cw), lambda i, c: (i, c)),
            pl.BlockSpec((tm, cw), lambda i, c: (i, c)),
            pl.BlockSpec((tm, LANES), lambda i, c: (i, 0)),
            pl.BlockSpec((tm, LANES), lambda i, c: (i, 0)),
        ],
        out_shape=[
            jax.ShapeDtypeStruct((t, GMLP_WIDTH), BF16),
            jax.ShapeDtypeStruct((t, CONV_CH), BF16),
            jax.ShapeDtypeStruct((t, LANES), F32),
            jax.ShapeDtypeStruct((t, LANES), F32),
        ],
        scratch_shapes=[
            pltpu.VMEM((tm, D_MODEL), BF16),
            pltpu.VMEM((n_groups, SUBLANES, cw), F32),
        ],
        compiler_params=pltpu.CompilerParams(
            dimension_semantics=("arbitrary", "arbitrary"),
            vmem_limit_bytes=VMEM_LIMIT),
    )(x2, x2, x2, x2, gmix, w_in, w_in, w_in, w_in, w_in, vng, ws, bsb, conv_taps)


def _lane_cumsum(v, lane):
    shift = 1
    while shift < N_EXPERTS:
        v = v + jnp.where(lane >= shift, pltpu.roll(v, shift, axis=1), 0.0)
        shift *= 2
    return v


def _lane_dense(col_values, lane):
    tile = jnp.zeros(lane.shape, F32)
    for k, col in enumerate(col_values):
        tile = jnp.where(lane == k, col, tile)
    return tile.T[0:SUBLANES, :]


def _finish_pages(n_page_rows, cnt_ref, ptab_ref, xs_ref, tab_ref, zero_scr, sem):
    page_shift = EXP_ROWS.bit_length() - 1
    n_slots = tab_ref.shape[1]
    zero_scr[...] = jnp.zeros_like(zero_scr)

    def clear(s, carry):
        for row in range(TAB_ROWS):
            tab_ref[row, s] = 0
        return carry

    lax.fori_loop(0, n_slots, clear, 0)

    def pages_of(e):
        return (cnt_ref[0, e] + (EXP_ROWS - 1)) >> page_shift

    def list_pages(e, carry):
        first_slot, run = carry

        def put(j, carry):
            tab_ref[TAB_ORDER, first_slot + j] = ptab_ref[e, j]
            tab_ref[TAB_EXPERT, first_slot + j] = e
            tab_ref[TAB_WSLOT, first_slot + j] = run & 1
            return carry

        n_pages_e = pages_of(e)
        lax.fori_loop(0, n_pages_e, put, 0)

        @pl.when(n_pages_e > 0)
        def _():
            tab_ref[TAB_RUN_LEN, first_slot] = n_pages_e

        return first_slot + n_pages_e, run + jnp.where(n_pages_e > 0, 1, 0)

    n_pages, _ = lax.fori_loop(0, N_EXPERTS, list_pages, (0, 0))
    tab_ref[TAB_N_PAGES, 0] = n_pages
    last_page = tab_ref[TAB_ORDER, n_pages - 1]
    last_expert = tab_ref[TAB_EXPERT, n_pages - 1]

    def repeat_last(s, carry):
        tab_ref[TAB_ORDER, s] = last_page
        tab_ref[TAB_EXPERT, s] = last_expert
        return carry

    lax.fori_loop(n_pages, n_slots, repeat_last, 0)

    def pad_copies(e):
        used = cnt_ref[0, e] & (EXP_ROWS - 1)
        last = ptab_ref[e, jnp.maximum(pages_of(e) - 1, 0)]
        return _zero_range_copies(zero_scr, xs_ref, sem, last * EXP_ROWS + used,
                                  (EXP_ROWS - used) & (EXP_ROWS - 1))

    def tail_copies(p):
        return [
            pltpu.make_async_copy(
                zero_scr,
                xs_ref.at[pl.ds(pl.multiple_of(p * EXP_ROWS + part * ZERO_ROWS, ZERO_ROWS),
                                ZERO_ROWS)],
                sem)
            for part in range(EXP_ROWS // ZERO_ROWS)
        ]

    def start_pads(e, carry):
        for needed, copy in pad_copies(e):
            pl.when(needed)(copy.start)
        return carry

    def wait_pads(e, carry):
        for needed, copy in pad_copies(e):
            pl.when(needed)(copy.wait)
        return carry

    def start_tail(p, carry):
        for copy in tail_copies(p):
            copy.start()
        return carry

    def wait_tail(p, carry):
        for copy in tail_copies(p):
            copy.wait()
        return carry

    total_pages = n_page_rows // EXP_ROWS
    lax.fori_loop(0, N_EXPERTS, start_pads, 0)
    lax.fori_loop(n_pages, total_pages, start_tail, 0)
    lax.fori_loop(0, N_EXPERTS, wait_pads, 0)
    lax.fori_loop(n_pages, total_pages, wait_tail, 0)


def _out_route_body(trash_row0, ya_ref, yb_ref, sa_ref, sb_ref, x_ref, ga_ref, gb_ref, wout_ref,
                    gffn_ref, wr_ref, br_ref, h_ref, xs_ref, tab_ref,
                    row_buf, pos_vmem, pos_smem, st_scr, ptab_scr, cnt_vmem, ptab_vmem,
                    cnt_smem, ptab_smem, zero_scr, row_sem, pos_sem, fill_sem):
    i = pl.program_id(0)
    n_steps = pl.num_programs(0) - 1
    tm = x_ref.shape[0]
    slot = lax.rem(i, 2)
    prev = 1 - slot

    def row_copy(r, k, prev=prev):
        return pltpu.make_async_copy(
            row_buf.at[prev, pl.ds(r, 1)], xs_ref.at[pl.ds(pos_smem[prev, k, r], 1)], row_sem)

    def wait_step_rows():
        for _ in range(2):
            pltpu.make_async_copy(row_buf.at[0], xs_ref.at[pl.ds(0, tm)], row_sem).wait()

    def pos_copy(s):
        return pltpu.make_async_copy(pos_vmem, pos_smem.at[s], pos_sem)

    @pl.when(i == 0)
    def _():
        st_scr[...] = jnp.zeros_like(st_scr)
        ptab_scr[...] = jnp.zeros_like(ptab_scr)
        row_buf[1] = jnp.zeros((tm, XS_COLS), F32)
        which = lax.broadcasted_iota(I32, (SUBLANES, tm), 0)
        pos_vmem[...] = (trash_row0 + lax.broadcasted_iota(I32, (SUBLANES, tm), 1)
                         + jnp.where(which == 1, tm, 0))
        pos_copy(1).start()

    @pl.when(i > 0)
    def _():
        wait_step_rows()

    def route_block(slot):
        prev = 1 - slot
        rows_per_chunk = tm // (D_MODEL // OUT_CHUNK)

        ra = _rms_scale(jnp.sum(sa_ref[...], axis=-1, keepdims=True), GMLP_WIDTH)
        rb = _rms_scale(jnp.sum(sb_ref[...], axis=-1, keepdims=True), CONV_CH)
        yna = (ya_ref[...].astype(F32) * ra * ga_ref[...]).astype(BF16)
        ynb = (yb_ref[...].astype(F32) * rb * gb_ref[...]).astype(BF16)
        yn = jnp.concatenate([yna, ynb], axis=1)
        for j in range(D_MODEL // OUT_CHUNK):
            cols = slice(j * OUT_CHUNK, (j + 1) * OUT_CHUNK)
            h_ref[:, cols] = x_ref[:, cols] + jnp.dot(
                yn, wout_ref[:, cols], preferred_element_type=F32)
            if j == 0:
                pos_copy(prev).wait()
            for r in range(j * rows_per_chunk, (j + 1) * rows_per_chunk):
                for k in range(2):
                    row_copy(r, k, prev).start(priority=k)
        h = h_ref[...]

        ms = jnp.mean(h * h, axis=-1, keepdims=True)
        hn = h * lax.rsqrt(ms + EPS) * gffn_ref[...]
        logits = jnp.dot(hn.astype(BF16), wr_ref[...], preferred_element_type=F32) + br_ref[...]

        lane = lax.broadcasted_iota(I32, (tm, LANES), 1)
        lane_f = lane.astype(F32)
        big = float(LANES)

        is_g = (lane >= GROUP_LANE0) & (lane < GROUP_LANE0 + N_GROUPS)
        lg = jnp.where(is_g, logits, MASKED)
        lg_top = jnp.max(lg, axis=-1, keepdims=True)
        g_idx = jnp.min(jnp.where(is_g & (lg == lg_top), lane_f - GROUP_LANE0, big),
                        axis=-1, keepdims=True).astype(I32)
        pg_top = 1.0 / jnp.sum(jnp.where(is_g, jnp.exp(lg - lg_top), 0.0), axis=-1, keepdims=True)

        group_shift = EXPERTS_PER_GROUP.bit_length() - 1
        in_grp = (lane < N_EXPERTS) & ((lane >> group_shift) == g_idx)
        le = jnp.where(in_grp, logits, MASKED)
        le1 = jnp.max(le, axis=-1, keepdims=True)
        e1 = jnp.min(jnp.where(in_grp & (le == le1), lane_f, big), axis=-1, keepdims=True)
        denom = jnp.sum(jnp.where(in_grp, jnp.exp(le - le1), 0.0), axis=-1, keepdims=True)
        rest = in_grp & (lane_f != e1)
        le2 = jnp.max(jnp.where(rest, le, MASKED), axis=-1, keepdims=True)
        e2 = jnp.min(jnp.where(rest & (le == le2), lane_f, big), axis=-1, keepdims=True)
        q1 = 1.0 / denom
        q2 = jnp.exp(le2 - le1) / denom
        qs = q1 + q2
        gate1 = pg_top * (q1 / qs)
        gate2 = pg_top * (q2 / qs)

        is1 = lane_f == e1
        is2 = lane_f == e2
        sel = jnp.where(is1 | is2, 1.0, 0.0)
        t_i = lax.broadcasted_iota(I32, (tm, tm), 0)
        t_j = lax.broadcasted_iota(I32, (tm, tm), 1)
        before = jnp.where(t_j < t_i, 1.0, 0.0).astype(BF16)
        run0 = st_scr[0:1, :]
        cur_page = st_scr[1:2, :]
        next_free = st_scr[2:3, :]
        cum = jnp.dot(before, sel.astype(BF16), preferred_element_type=F32) + run0

        lane_row = lax.broadcasted_iota(I32, (1, LANES), 1)
        run1 = run0 + jnp.sum(sel, axis=0, keepdims=True)
        pages0 = jnp.ceil(run0 / EXP_ROWS)
        n_new = jnp.ceil(run1 / EXP_ROWS) - pages0
        first_new = next_free + _lane_cumsum(n_new, lane_row) - n_new

        def pick(mask, per_expert):
            return jnp.sum(jnp.where(mask, per_expert, 0.0), axis=-1, keepdims=True)

        def place(mask):
            rank = pick(mask, cum)
            page_idx = jnp.floor(rank / EXP_ROWS)
            owned = pick(mask, pages0)
            page = jnp.where(page_idx < owned, pick(mask, cur_page),
                             pick(mask, first_new) + (page_idx - owned))
            return page * EXP_ROWS + (rank - page_idx * EXP_ROWS)

        pos1 = place(is1)
        pos2 = place(is2)

        st_scr[0:1, :] = run1
        st_scr[1:2, :] = jnp.where(n_new > 0, first_new + n_new - 1, cur_page)
        st_scr[2:3, :] = next_free + jnp.sum(n_new, axis=-1, keepdims=True)

        per_expert = jnp.concatenate(
            [pages0, n_new, first_new, jnp.zeros((LANES - 3, LANES), F32)], axis=0).T
        owned_c = per_expert[0:N_EXPERTS, 0:1]
        n_new_c = per_expert[0:N_EXPERTS, 1:2]
        first_c = per_expert[0:N_EXPERTS, 2:3]
        idx = lax.broadcasted_iota(I32, (N_EXPERTS, LANES), 1).astype(F32)
        fresh = (idx >= owned_c) & (idx < owned_c + n_new_c)
        ptab_scr[...] = jnp.where(fresh, first_c + (idx - owned_c), ptab_scr[...])
        cnt_vmem[...] = jnp.broadcast_to(run1, cnt_vmem.shape).astype(I32)
        ptab_vmem[...] = ptab_scr[...].astype(I32)

        token = (i * tm + lax.broadcasted_iota(I32, (tm, 1), 0) + 1).astype(F32)
        meta = jnp.zeros((tm, LANES), F32)
        for k, col in ((META_TOKEN, token), (META_GATE0, gate1), (META_POS0, pos1),
                       (META_GATE1, gate2), (META_POS1, pos2)):
            meta = jnp.where(lane == k, col, meta)
        row_buf[slot, :, 0:D_MODEL] = hn
        row_buf[slot, :, D_MODEL:XS_COLS] = meta
        pos_vmem[...] = _lane_dense([pos1, pos2], lane).astype(I32)
        pos_copy(slot).start()

    for static_slot in range(2):
        pl.when((i < n_steps) & (slot == static_slot))(
            functools.partial(route_block, static_slot))

    @pl.when(i == n_steps)
    def _():
        def issue(r, carry):
            for k in range(2):
                row_copy(r, k).start(priority=k)
            return carry

        pos_copy(prev).wait()
        lax.fori_loop(0, tm, issue, 0, unroll=8)
        wait_step_rows()

        for vec, scal in ((cnt_vmem, cnt_smem), (ptab_vmem, ptab_smem)):
            copy = pltpu.make_async_copy(vec, scal, fill_sem)
            copy.start()
            copy.wait()
        _finish_pages(trash_row0, cnt_smem, ptab_smem, xs_ref, tab_ref, zero_scr, fill_sem)


def _out_route(ya, yb, sa, sb, x2, ga, gb, w_out, gffn, wr, br, n_page_rows):
    t = x2.shape[0]
    tm = OUT_ROWS
    n_steps = t // tm
    const = lambda i: (0, 0)
    rows = lambda i: (jnp.minimum(i, n_steps - 1), 0)
    return pl.pallas_call(
        functools.partial(_out_route_body, n_page_rows),
        name="out_route",
        grid=(n_steps + 1,),
        in_specs=[
            pl.BlockSpec((tm, GMLP_WIDTH), rows),
            pl.BlockSpec((tm, CONV_CH), rows),
            pl.BlockSpec((tm, LANES), rows),
            pl.BlockSpec((tm, LANES), rows),
            pl.BlockSpec((tm, D_MODEL), rows),
            pl.BlockSpec((1, GMLP_WIDTH), const),
            pl.BlockSpec((1, CONV_CH), const),
            pl.BlockSpec((D_MODEL, D_MODEL), const, pipeline_mode=pl.Buffered(1)),
            pl.BlockSpec((1, D_MODEL), const),
            pl.BlockSpec((D_MODEL, LANES), const),
            pl.BlockSpec((1, LANES), const),
        ],
        out_specs=[
            pl.BlockSpec((tm, D_MODEL), rows),
            pl.BlockSpec(memory_space=pl.ANY),
            pl.BlockSpec(memory_space=pltpu.SMEM),
        ],
        out_shape=[
            jax.ShapeDtypeStruct((t, D_MODEL), F32),
            jax.ShapeDtypeStruct((n_page_rows + 2 * tm, XS_COLS), F32),
            jax.ShapeDtypeStruct((TAB_ROWS, TAB_LANES), I32),
        ],
        scratch_shapes=[
            pltpu.VMEM((2, tm, XS_COLS), F32),
            pltpu.VMEM((SUBLANES, tm), I32),
            pltpu.SMEM((2, SUBLANES, tm), I32),
            pltpu.VMEM((SUBLANES, LANES), F32),
            pltpu.VMEM((N_EXPERTS, LANES), F32),
            pltpu.VMEM((SUBLANES, LANES), I32),
            pltpu.VMEM((N_EXPERTS, LANES), I32),
            pltpu.SMEM((SUBLANES, LANES), I32),
            pltpu.SMEM((N_EXPERTS, LANES), I32),
            pltpu.VMEM((ZERO_ROWS, XS_COLS), F32),
            pltpu.SemaphoreType.DMA,
            pltpu.SemaphoreType.DMA,
            pltpu.SemaphoreType.DMA,
        ],
        compiler_params=pltpu.CompilerParams(
            dimension_semantics=("arbitrary",),
            vmem_limit_bytes=VMEM_LIMIT),
    )(ya, yb, sa, sb, x2, ga, gb, w_out, gffn, wr, br)


def _zero_range_copies(zero_scr, xs_ref, sem, start, length):
    head = length & (SUBLANES - 1)
    copies = []
    for j in range(SUBLANES - 1):
        copy = pltpu.make_async_copy(
            zero_scr.at[pl.ds(0, 1)], xs_ref.at[pl.ds(start + j, 1)], sem)
        copies.append((j < head, copy))
    body_start = start + head
    body = length - head
    size = ZERO_ROWS
    while size >= SUBLANES:
        offset = pl.multiple_of(body_start + (body & ~(2 * size - 1)), SUBLANES)
        copy = pltpu.make_async_copy(
            zero_scr.at[pl.ds(0, size)], xs_ref.at[pl.ds(offset, size)], sem)
        copies.append(((body & size) != 0, copy))
        size //= 2
    return copies


def _expert_mlp_body(n_tokens, tab_ref, xs_ref, wg_hbm, wu_hbm, wd_hbm, out2_ref,
                     wg_f32, wu_f32, wd_f32, wg_scr, wu_scr, wd_scr, out_buf, dest_vmem,
                     dest_smem, row_sem, dest_sem, w_sem):
    s = pl.program_id(0)
    n_pages = tab_ref[TAB_N_PAGES, 0]
    bm = xs_ref.shape[0]
    cur = lax.rem(s, 3)
    prv = lax.rem(s + 2, 3)
    par = lax.rem(s, 2)
    ppar = 1 - par
    trash_row0 = 2 * n_tokens
    expert = tab_ref[TAB_EXPERT, s]
    run_len = tab_ref[TAB_RUN_LEN, s]
    wslot = tab_ref[TAB_WSLOT, s]

    def weight_copies(e, slot):
        return [
            pltpu.make_async_copy(src.at[e], dst.at[slot], w_sem.at[slot])
            for src, dst in ((wg_hbm, wg_f32), (wu_hbm, wu_f32), (wd_hbm, wd_f32))
        ]

    def row_copy(r, prv=prv):
        return pltpu.make_async_copy(
            out_buf.at[prv, pl.ds(r, 1)], out2_ref.at[pl.ds(dest_smem[ppar, 0, r], 1)],
            row_sem.at[prv])

    def wait_rows(slot):
        pltpu.make_async_copy(out_buf.at[0], out2_ref.at[pl.ds(0, bm)], row_sem.at[slot]).wait()

    def dest_copy(p):
        return pltpu.make_async_copy(dest_vmem, dest_smem.at[p], dest_sem)

    @pl.when(s == 0)
    def _():
        out_buf[2] = jnp.zeros((bm, D_MODEL), F32)
        for half in range(2):
            clear = pltpu.make_async_copy(
                out_buf.at[2], out2_ref.at[pl.ds(trash_row0 + half * bm, bm)], dest_sem)
            clear.start()
            clear.wait()

        def fill(r, carry):
            dest_smem[1, 0, r] = trash_row0 + bm + r
            return carry

        lax.fori_loop(0, bm, fill, 0)
        for copy in weight_copies(expert, wslot):
            copy.start()

    @pl.when((s >= 2) & (s <= n_pages))
    def _():
        wait_rows(cur)

    @pl.when((s >= 1) & (s <= n_pages))
    def _():
        dest_copy(ppar).wait()

    @pl.when((s < n_pages) & (run_len > 0))
    def _():
        for copy in weight_copies(expert, wslot):
            copy.wait()
        wg_scr[...] = wg_f32[wslot].astype(BF16)
        wu_scr[...] = wu_f32[wslot].astype(BF16)
        wd_scr[...] = wd_f32[wslot].astype(BF16)
        next_run = s + run_len

        @pl.when(next_run < n_pages)
        def _():
            for copy in weight_copies(tab_ref[TAB_EXPERT, next_run], 1 - wslot):
                copy.start()

    def compute_page(cur):
        prv = (cur + 2) % 3
        down_parts = 2
        n_groups = 2 + down_parts
        group_rows = bm // n_groups
        part_d = D_MODEL // down_parts

        def scatter_group(g):
            for r in range(g * group_rows, (g + 1) * group_rows):
                row_copy(r, prv).start(priority=r % 2)

        x = xs_ref[:, 0:D_MODEL].astype(BF16)
        gate = jnp.dot(x, wg_scr[...], preferred_element_type=F32)

        meta = xs_ref[:, D_MODEL:XS_COLS]
        own_row = (tab_ref[TAB_ORDER, s] * bm
                   + lax.broadcasted_iota(I32, (bm, 1), 0)).astype(F32)
        local = lax.broadcasted_iota(I32, (bm, 1), 0).astype(F32)
        token = meta[:, META_TOKEN:META_TOKEN + 1]
        first = meta[:, META_POS0:META_POS0 + 1] == own_row
        weight = jnp.where(first, meta[:, META_GATE0:META_GATE0 + 1],
                           meta[:, META_GATE1:META_GATE1 + 1])
        plane = jnp.where(first, 0.0, float(n_tokens))
        dest = jnp.where(token > 0.0, plane + token - 1.0,
                         (trash_row0 + par * bm).astype(F32) + local)
        lane = lax.broadcasted_iota(I32, (bm, LANES), 1)
        dest_vmem[...] = _lane_dense([dest], lane).astype(I32)
        dest_copy(par).start()
        scatter_group(0)
        up = jnp.dot(x, wu_scr[...], preferred_element_type=F32)
        scatter_group(1)
        hidden = (jax.nn.silu(gate) * up).astype(BF16)

        for part in range(down_parts):
            cols = slice(part * part_d, (part + 1) * part_d)
            out_buf[cur, :, cols] = jnp.dot(
                hidden, wd_scr[:, cols], preferred_element_type=F32) * weight
            scatter_group(2 + part)

    for static_cur in range(3):
        pl.when((s < n_pages) & (cur == static_cur))(
            functools.partial(compute_page, static_cur))

    @pl.when(s == n_pages)
    def _():
        def issue(pair, carry):
            for k in range(2):
                row_copy(2 * pair + k).start(priority=k)
            return carry

        lax.fori_loop(0, bm // 2, issue, 0, unroll=4)
        wait_rows(lax.rem(s + 1, 3))
        wait_rows(prv)


def _expert_mlp(tab, xs, w_gate, w_up, w_down, n_tokens, n_pages_max):
    bm = EXP_ROWS
    page_of = lambda s, tab: (tab[TAB_ORDER, s], 0)
    return pl.pallas_call(
        functools.partial(_expert_mlp_body, n_tokens),
        name="expert_mlp",
        grid_spec=pltpu.PrefetchScalarGridSpec(
            num_scalar_prefetch=1,
            grid=(n_pages_max + 1,),
            in_specs=[
                pl.BlockSpec((bm, XS_COLS), page_of),
                pl.BlockSpec(memory_space=pl.ANY),
                pl.BlockSpec(memory_space=pl.ANY),
                pl.BlockSpec(memory_space=pl.ANY),
            ],
            out_specs=pl.BlockSpec(memory_space=pl.ANY),
            scratch_shapes=[
                pltpu.VMEM((2, D_MODEL, D_EXPERT), F32),
                pltpu.VMEM((2, D_MODEL, D_EXPERT), F32),
                pltpu.VMEM((2, D_EXPERT, D_MODEL), F32),
                pltpu.VMEM((D_MODEL, D_EXPERT), BF16),
                pltpu.VMEM((D_MODEL, D_EXPERT), BF16),
                pltpu.VMEM((D_EXPERT, D_MODEL), BF16),
                pltpu.VMEM((3, bm, D_MODEL), F32),
                pltpu.VMEM((SUBLANES, bm), I32),
                pltpu.SMEM((2, SUBLANES, bm), I32),
                pltpu.SemaphoreType.DMA((3,)),
                pltpu.SemaphoreType.DMA,
                pltpu.SemaphoreType.DMA((2,)),
            ],
        ),
        out_shape=jax.ShapeDtypeStruct((2 * n_tokens + 2 * bm, D_MODEL), F32),
        compiler_params=pltpu.CompilerParams(
            dimension_semantics=("arbitrary",),
            vmem_limit_bytes=VMEM_LIMIT),
    )(tab, xs, w_gate, w_up, w_down)


def _final_norm_body(h_ref, y0_ref, y1_ref, gfin_ref, out_ref):
    h = h_ref[...] + (y0_ref[...] + y1_ref[...])
    ms = jnp.mean(h * h, axis=-1, keepdims=True)
    out_ref[...] = h * lax.rsqrt(ms + EPS) * gfin_ref[...]


def _final_norm(h, out2, gfin):
    t = h.shape[0]
    tm = FIN_ROWS
    plane1 = t // tm
    return pl.pallas_call(
        _final_norm_body,
        name="final_norm",
        grid=(t // tm,),
        in_specs=[
            pl.BlockSpec((tm, D_MODEL), lambda i: (i, 0)),
            pl.BlockSpec((tm, D_MODEL), lambda i: (i, 0)),
            pl.BlockSpec((tm, D_MODEL), lambda i: (i + plane1, 0)),
            pl.BlockSpec((1, D_MODEL), lambda i: (0, 0)),
        ],
        out_specs=pl.BlockSpec((tm, D_MODEL), lambda i: (i, 0)),
        out_shape=jax.ShapeDtypeStruct((t, D_MODEL), F32),
        compiler_params=pltpu.CompilerParams(
            dimension_semantics=("arbitrary",),
            vmem_limit_bytes=VMEM_LIMIT),
    )(h, out2, out2, gfin)


def kernel(x, norm_mix_g, w_in, gmlp_v_norm_g, gmlp_ws, gmlp_bs, conv_w, out_norm_gmlp_g,
           out_norm_conv_g, w_out, norm_ffn_g, router_group_w, router_group_b, router_expert_w,
           router_expert_b, expert_w_gate, expert_w_up, expert_w_down, norm_final_g):
    batch, seq_len, d_model = x.shape
    t = batch * seq_len
    assert w_in.shape[0] == 1, "single-layer block"
    assert t // EXP_ROWS <= LANES, "page table holds at most LANES pages per expert"
    n_col_groups = GMLP_WIDTH // MIX_COLS
    n_pages_max = 2 * t // EXP_ROWS + N_EXPERTS
    n_page_rows = n_pages_max * EXP_ROWS
    x2 = x.reshape(t, d_model)

    vng = gmlp_v_norm_g[0].reshape(n_col_groups, 1, MIX_COLS)
    bsb = jnp.broadcast_to(gmlp_bs[0][:, :, None], (GMLP_HEADS, GMLP_BLOCK, HEAD_DIM))
    taps = conv_w[0].reshape(CONV_K, n_col_groups, MIX_COLS).transpose(1, 0, 2)
    unused = LANES - N_EXPERTS - N_GROUPS
    wr = jnp.concatenate(
        [router_expert_w[0], router_group_w[0], jnp.zeros((d_model, unused), F32)],
        axis=1).astype(BF16)
    br = jnp.concatenate(
        [router_expert_b[0], router_group_b[0], jnp.zeros((unused,), F32)])[None, :]

    ya, yb, sa, sb = _mixer_proj(x2, norm_mix_g[0][None, :], w_in[0].astype(BF16), vng,
                                 gmlp_ws[0], bsb, taps, seq_len)
    h, xs, tab = _out_route(
        ya, yb, sa, sb, x2, out_norm_gmlp_g[0][None, :], out_norm_conv_g[0][None, :],
        w_out[0].astype(BF16), norm_ffn_g[0][None, :], wr, br, n_page_rows)
    out2 = _expert_mlp(tab, xs, expert_w_gate[0], expert_w_up[0], expert_w_down[0],
                       t, n_pages_max)
    out = _final_norm(h, out2, norm_final_g[None, :])
    return out.reshape(batch, seq_len, d_model)
```

```python
import functools

import jax
import jax.numpy as jnp
from jax import lax
from jax.experimental import pallas as pl
from jax.experimental.pallas import tpu as pltpu

F32 = jnp.float32
BF16 = jnp.bfloat16
I32 = jnp.int32

D_MODEL = 2048
CHUNK = 64
GMLP_WIDTH = 1024
GMLP_HEADS = 8
HEAD_DIM = 128
GMLP_BLOCK = 128
CONV_CH = 1024
CONV_K = 3
N_GROUPS = 4
EXPERTS_PER_GROUP = 8
N_EXPERTS = 32
D_EXPERT = 512
EPS = 1e-6

LANES = 128
SUBLANES = 8

MIX_ROWS = 1024
MIX_COLS = 256
OUT_ROWS = 512
OUT_CHUNK = 512
EXP_ROWS = 256
FIN_ROWS = 512
ZERO_ROWS = 128
GROUP_LANE0 = N_EXPERTS
MASKED = -1e30
VMEM_LIMIT = 56 * 1024 * 1024

XS_COLS = D_MODEL + LANES
META_TOKEN = 0
META_GATE0 = 1
META_POS0 = 2
META_GATE1 = 3
META_POS1 = 4

TAB_ORDER = 0
TAB_EXPERT = 1
TAB_N_PAGES = 2
TAB_RUN_LEN = 3
TAB_WSLOT = 4
TAB_ROWS = 8
TAB_LANES = 256


def _rms_scale(sumsq, width):
    return lax.rsqrt(sumsq / width + EPS)


def _mixer_proj_body(seq_len, x_ref, gmix_ref, wu_ref, wv_ref, wbg_ref, wcg_ref, whv_ref,
                     vng_ref, ws_ref, bsb_ref, cw_ref,
                     ya_ref, yb_ref, sa_ref, sb_ref, xn_scr, carry_scr):
    i = pl.program_id(0)
    c = pl.program_id(1)
    tm = x_ref.shape[0]
    cw = ya_ref.shape[1]
    heads_per_step = cw // HEAD_DIM

    @pl.when(c == 0)
    def _():
        x = x_ref[...]
        ms = jnp.mean(x * x, axis=-1, keepdims=True)
        xn_scr[...] = (x * lax.rsqrt(ms + EPS) * gmix_ref[...]).astype(BF16)
        sa_ref[...] = jnp.zeros_like(sa_ref)
        sb_ref[...] = jnp.zeros_like(sb_ref)

    xn = xn_scr[...]
    project = lambda w_ref: jnp.dot(xn, w_ref[...], preferred_element_type=F32)
    u = jax.nn.gelu(project(wu_ref))
    v = jax.nn.gelu(project(wv_ref))
    bg = project(wbg_ref)
    cg = project(wcg_ref)
    hv = project(whv_ref)

    pos_i = lax.broadcasted_iota(I32, (GMLP_BLOCK, GMLP_BLOCK), 0)
    pos_j = lax.broadcasted_iota(I32, (GMLP_BLOCK, GMLP_BLOCK), 1)
    chunk_shift = CHUNK.bit_length() - 1
    causal = (pos_i >> chunk_shift) >= (pos_j >> chunk_shift)
    vng = vng_ref[0]
    ya_heads = []
    for j in range(heads_per_step):
        head = c * heads_per_step + j
        lanes = slice(j * HEAD_DIM, (j + 1) * HEAD_DIM)
        vj = v[:, lanes]
        ms = jnp.mean(vj * vj, axis=-1, keepdims=True)
        vn = (vj * lax.rsqrt(ms + EPS) * vng[:, lanes]).astype(BF16)
        w_mix = jnp.where(causal, ws_ref[head], 0.0).astype(BF16)
        bias = bsb_ref[head]
        mixed = [
            jnp.dot(w_mix, vn[p * GMLP_BLOCK:(p + 1) * GMLP_BLOCK, :],
                    preferred_element_type=F32) + bias
            for p in range(tm // GMLP_BLOCK)
        ]
        ya_heads.append(u[:, lanes] * jnp.concatenate(mixed, axis=0))
    ya = jnp.concatenate(ya_heads, axis=1) if heads_per_step > 1 else ya_heads[0]

    z = cg * hv
    prev = carry_scr[c]
    seq_start = (i * tm) % seq_len == 0
    prev = jnp.where(seq_start, 0.0, prev)
    row = lax.broadcasted_iota(I32, (tm, cw), 0)
    z1 = jnp.where(row == 0, prev[SUBLANES - 1:SUBLANES, :], pltpu.roll(z, 1, axis=0))
    z2 = jnp.where(row == 0, prev[SUBLANES - 2:SUBLANES - 1, :],
                   jnp.where(row == 1, prev[SUBLANES - 1:SUBLANES, :],
                             pltpu.roll(z, 2, axis=0)))
    taps = cw_ref[0]
    conv = taps[0:1, :] * z2 + taps[1:2, :] * z1 + taps[2:3, :] * z
    yb = bg * conv
    carry_scr[c] = z[tm - SUBLANES:tm, :]

    ya_ref[...] = ya.astype(BF16)
    yb_ref[...] = yb.astype(BF16)
    ya2 = ya * ya
    yb2 = yb * yb
    sa_ref[...] += sum(ya2[:, k * LANES:(k + 1) * LANES] for k in range(cw // LANES))
    sb_ref[...] += sum(yb2[:, k * LANES:(k + 1) * LANES] for k in range(cw // LANES))


def _mixer_proj(x2, gmix, w_in, vng, ws, bsb, conv_taps, seq_len):
    t = x2.shape[0]
    cw = MIX_COLS
    tm = MIX_ROWS
    n_groups = GMLP_WIDTH // cw
    grid = (t // tm, n_groups)
    w_part = lambda k: pl.BlockSpec((D_MODEL, cw), lambda i, c: (0, k * n_groups + c))
    return pl.pallas_call(
        functools.partial(_mixer_proj_body, seq_len),
        name="mixer_proj",
        grid=grid,
        in_specs=[
            pl.BlockSpec((tm, D_MODEL), lambda i, c: (i, 0)),
            pl.BlockSpec((1, D_MODEL), lambda i, c: (0, 0)),
            w_part(0), w_part(1), w_part(2), w_part(3), w_part(4),
            pl.BlockSpec((1, 1, cw), lambda i, c: (c, 0, 0)),
            pl.BlockSpec((GMLP_HEADS, GMLP_BLOCK, GMLP_BLOCK), lambda i, c: (0, 0, 0)),
            pl.BlockSpec((GMLP_HEADS, GMLP_BLOCK, HEAD_DIM), lambda i, c: (0, 0, 0)),
            pl.BlockSpec((1, CONV_K, cw), lambda i, c: (c, 0, 0)),
        ],
        out_specs=[
            pl.BlockSpec((tm, cw), lambda i, c: (i, c)),
            pl.BlockSpec((tm, cw), lambda i, c: (i, c)),
            pl.BlockSpec((tm, LANES), lambda i, c: (i, 0)),
            pl.BlockSpec((tm, LANES), lambda i, c: (i, 0)),
        ],
        out_shape=[
            jax.ShapeDtypeStruct((t, GMLP_WIDTH), BF16),
            jax.ShapeDtypeStruct((t, CONV_CH), BF16),
            jax.ShapeDtypeStruct((t, LANES), F32),
            jax.ShapeDtypeStruct((t, LANES), F32),
        ],
        scratch_shapes=[
            pltpu.VMEM((tm, D_MODEL), BF16),
            pltpu.VMEM((n_groups, SUBLANES, cw), F32),
        ],
        compiler_params=pltpu.CompilerParams(
            dimension_semantics=("arbitrary", "arbitrary"),
            vmem_limit_bytes=VMEM_LIMIT),
    )(x2, gmix, w_in, w_in, w_in, w_in, w_in, vng, ws, bsb, conv_taps)


def _lane_cumsum(v, lane):
    shift = 1
    while shift < N_EXPERTS:
        v = v + jnp.where(lane >= shift, pltpu.roll(v, shift, axis=1), 0.0)
        shift *= 2
    return v


def _lane_dense(col_values, lane):
    tile = jnp.zeros(lane.shape, F32)
    for k, col in enumerate(col_values):
        tile = jnp.where(lane == k, col, tile)
    return tile.T[0:SUBLANES, :]


def _finish_pages(n_page_rows, cnt_ref, ptab_ref, xs_ref, tab_ref, zero_scr, sem):
    page_shift = EXP_ROWS.bit_length() - 1
    n_slots = tab_ref.shape[1]
    zero_scr[...] = jnp.zeros_like(zero_scr)

    def clear(s, carry):
        for row in range(TAB_ROWS):
            tab_ref[row, s] = 0
        return carry

    lax.fori_loop(0, n_slots, clear, 0)

    def pages_of(e):
        return (cnt_ref[0, e] + (EXP_ROWS - 1)) >> page_shift

    def list_pages(e, carry):
        first_slot, run = carry

        def put(j, carry):
            tab_ref[TAB_ORDER, first_slot + j] = ptab_ref[e, j]
            tab_ref[TAB_EXPERT, first_slot + j] = e
            tab_ref[TAB_WSLOT, first_slot + j] = run & 1
            return carry

        n_pages_e = pages_of(e)
        lax.fori_loop(0, n_pages_e, put, 0)

        @pl.when(n_pages_e > 0)
        def _():
            tab_ref[TAB_RUN_LEN, first_slot] = n_pages_e

        return first_slot + n_pages_e, run + jnp.where(n_pages_e > 0, 1, 0)

    n_pages, _ = lax.fori_loop(0, N_EXPERTS, list_pages, (0, 0))
    tab_ref[TAB_N_PAGES, 0] = n_pages
    last_page = tab_ref[TAB_ORDER, n_pages - 1]
    last_expert = tab_ref[TAB_EXPERT, n_pages - 1]

    def repeat_last(s, carry):
        tab_ref[TAB_ORDER, s] = last_page
        tab_ref[TAB_EXPERT, s] = last_expert
        return carry

    lax.fori_loop(n_pages, n_slots, repeat_last, 0)

    def pad_copies(e):
        used = cnt_ref[0, e] & (EXP_ROWS - 1)
        last = ptab_ref[e, jnp.maximum(pages_of(e) - 1, 0)]
        return _zero_range_copies(zero_scr, xs_ref, sem, last * EXP_ROWS + used,
                                  (EXP_ROWS - used) & (EXP_ROWS - 1))

    def tail_copies(p):
        return [
            pltpu.make_async_copy(
                zero_scr,
                xs_ref.at[pl.ds(pl.multiple_of(p * EXP_ROWS + part * ZERO_ROWS, ZERO_ROWS),
                                ZERO_ROWS)],
                sem)
            for part in range(EXP_ROWS // ZERO_ROWS)
        ]

    def start_pads(e, carry):
        for needed, copy in pad_copies(e):
            pl.when(needed)(copy.start)
        return carry

    def wait_pads(e, carry):
        for needed, copy in pad_copies(e):
            pl.when(needed)(copy.wait)
        return carry

    def start_tail(p, carry):
        for copy in tail_copies(p):
            copy.start()
        return carry

    def wait_tail(p, carry):
        for copy in tail_copies(p):
            copy.wait()
        return carry

    total_pages = n_page_rows // EXP_ROWS
    lax.fori_loop(0, N_EXPERTS, start_pads, 0)
    lax.fori_loop(n_pages, total_pages, start_tail, 0)
    lax.fori_loop(0, N_EXPERTS, wait_pads, 0)
    lax.fori_loop(n_pages, total_pages, wait_tail, 0)


def _out_route_body(trash_row0, ya_ref, yb_ref, sa_ref, sb_ref, x_ref, ga_ref, gb_ref, wout_ref,
                    gffn_ref, wr_ref, br_ref, h_ref, xs_ref, tab_ref,
                    row_buf, pos_vmem, pos_smem, st_scr, ptab_scr, cnt_vmem, ptab_vmem,
                    cnt_smem, ptab_smem, zero_scr, row_sem, pos_sem, fill_sem):
    i = pl.program_id(0)
    n_steps = pl.num_programs(0) - 1
    tm = x_ref.shape[0]
    slot = lax.rem(i, 2)
    prev = 1 - slot

    def row_copy(r, k, prev=prev):
        return pltpu.make_async_copy(
            row_buf.at[prev, pl.ds(r, 1)], xs_ref.at[pl.ds(pos_smem[prev, k, r], 1)], row_sem)

    def wait_step_rows():
        for _ in range(2):
            pltpu.make_async_copy(row_buf.at[0], xs_ref.at[pl.ds(0, tm)], row_sem).wait()

    def pos_copy(s):
        return pltpu.make_async_copy(pos_vmem, pos_smem.at[s], pos_sem)

    @pl.when(i == 0)
    def _():
        st_scr[...] = jnp.zeros_like(st_scr)
        ptab_scr[...] = jnp.zeros_like(ptab_scr)
        row_buf[1] = jnp.zeros((tm, XS_COLS), F32)
        which = lax.broadcasted_iota(I32, (SUBLANES, tm), 0)
        pos_vmem[...] = (trash_row0 + lax.broadcasted_iota(I32, (SUBLANES, tm), 1)
                         + jnp.where(which == 1, tm, 0))
        pos_copy(1).start()

    @pl.when(i > 0)
    def _():
        wait_step_rows()

    def route_block(slot):
        prev = 1 - slot
        rows_per_chunk = tm // (D_MODEL // OUT_CHUNK)

        ra = _rms_scale(jnp.sum(sa_ref[...], axis=-1, keepdims=True), GMLP_WIDTH)
        rb = _rms_scale(jnp.sum(sb_ref[...], axis=-1, keepdims=True), CONV_CH)
        yna = (ya_ref[...].astype(F32) * ra * ga_ref[...]).astype(BF16)
        ynb = (yb_ref[...].astype(F32) * rb * gb_ref[...]).astype(BF16)
        yn = jnp.concatenate([yna, ynb], axis=1)
        for j in range(D_MODEL // OUT_CHUNK):
            cols = slice(j * OUT_CHUNK, (j + 1) * OUT_CHUNK)
            h_ref[:, cols] = x_ref[:, cols] + jnp.dot(
                yn, wout_ref[:, cols], preferred_element_type=F32)
            if j == 0:
                pos_copy(prev).wait()
            for r in range(j * rows_per_chunk, (j + 1) * rows_per_chunk):
                for k in range(2):
                    row_copy(r, k, prev).start(priority=k)
        h = h_ref[...]

        ms = jnp.mean(h * h, axis=-1, keepdims=True)
        hn = h * lax.rsqrt(ms + EPS) * gffn_ref[...]
        logits = jnp.dot(hn.astype(BF16), wr_ref[...], preferred_element_type=F32) + br_ref[...]

        lane = lax.broadcasted_iota(I32, (tm, LANES), 1)
        lane_f = lane.astype(F32)
        big = float(LANES)

        is_g = (lane >= GROUP_LANE0) & (lane < GROUP_LANE0 + N_GROUPS)
        lg = jnp.where(is_g, logits, MASKED)
        lg_top = jnp.max(lg, axis=-1, keepdims=True)
        g_idx = jnp.min(jnp.where(is_g & (lg == lg_top), lane_f - GROUP_LANE0, big),
                        axis=-1, keepdims=True).astype(I32)
        pg_top = 1.0 / jnp.sum(jnp.where(is_g, jnp.exp(lg - lg_top), 0.0), axis=-1, keepdims=True)

        group_shift = EXPERTS_PER_GROUP.bit_length() - 1
        in_grp = (lane < N_EXPERTS) & ((lane >> group_shift) == g_idx)
        le = jnp.where(in_grp, logits, MASKED)
        le1 = jnp.max(le, axis=-1, keepdims=True)
        e1 = jnp.min(jnp.where(in_grp & (le == le1), lane_f, big), axis=-1, keepdims=True)
        denom = jnp.sum(jnp.where(in_grp, jnp.exp(le - le1), 0.0), axis=-1, keepdims=True)
        rest = in_grp & (lane_f != e1)
        le2 = jnp.max(jnp.where(rest, le, MASKED), axis=-1, keepdims=True)
        e2 = jnp.min(jnp.where(rest & (le == le2), lane_f, big), axis=-1, keepdims=True)
        q1 = 1.0 / denom
        q2 = jnp.exp(le2 - le1) / denom
        qs = q1 + q2
        gate1 = pg_top * (q1 / qs)
        gate2 = pg_top * (q2 / qs)

        is1 = lane_f == e1
        is2 = lane_f == e2
        sel = jnp.where(is1 | is2, 1.0, 0.0)
        t_i = lax.broadcasted_iota(I32, (tm, tm), 0)
        t_j = lax.broadcasted_iota(I32, (tm, tm), 1)
        before = jnp.where(t_j < t_i, 1.0, 0.0).astype(BF16)
        run0 = st_scr[0:1, :]
        cur_page = st_scr[1:2, :]
        next_free = st_scr[2:3, :]
        cum = jnp.dot(before, sel.astype(BF16), preferred_element_type=F32) + run0

        lane_row = lax.broadcasted_iota(I32, (1, LANES), 1)
        run1 = run0 + jnp.sum(sel, axis=0, keepdims=True)
        pages0 = jnp.ceil(run0 / EXP_ROWS)
        n_new = jnp.ceil(run1 / EXP_ROWS) - pages0
        first_new = next_free + _lane_cumsum(n_new, lane_row) - n_new

        def pick(mask, per_expert):
            return jnp.sum(jnp.where(mask, per_expert, 0.0), axis=-1, keepdims=True)

        def place(mask):
            rank = pick(mask, cum)
            page_idx = jnp.floor(rank / EXP_ROWS)
            owned = pick(mask, pages0)
            page = jnp.where(page_idx < owned, pick(mask, cur_page),
                             pick(mask, first_new) + (page_idx - owned))
            return page * EXP_ROWS + (rank - page_idx * EXP_ROWS)

        pos1 = place(is1)
        pos2 = place(is2)

        st_scr[0:1, :] = run1
        st_scr[1:2, :] = jnp.where(n_new > 0, first_new + n_new - 1, cur_page)
        st_scr[2:3, :] = next_free + jnp.sum(n_new, axis=-1, keepdims=True)

        per_expert = jnp.concatenate(
            [pages0, n_new, first_new, jnp.zeros((LANES - 3, LANES), F32)], axis=0).T
        owned_c = per_expert[0:N_EXPERTS, 0:1]
        n_new_c = per_expert[0:N_EXPERTS, 1:2]
        first_c = per_expert[0:N_EXPERTS, 2:3]
        idx = lax.broadcasted_iota(I32, (N_EXPERTS, LANES), 1).astype(F32)
        fresh = (idx >= owned_c) & (idx < owned_c + n_new_c)
        ptab_scr[...] = jnp.where(fresh, first_c + (idx - owned_c), ptab_scr[...])
        cnt_vmem[...] = jnp.broadcast_to(run1, cnt_vmem.shape).astype(I32)
        ptab_vmem[...] = ptab_scr[...].astype(I32)

        token = (i * tm + lax.broadcasted_iota(I32, (tm, 1), 0) + 1).astype(F32)
        meta = jnp.zeros((tm, LANES), F32)
        for k, col in ((META_TOKEN, token), (META_GATE0, gate1), (META_POS0, pos1),
                       (META_GATE1, gate2), (META_POS1, pos2)):
            meta = jnp.where(lane == k, col, meta)
        row_buf[slot, :, 0:D_MODEL] = hn
        row_buf[slot, :, D_MODEL:XS_COLS] = meta
        pos_vmem[...] = _lane_dense([pos1, pos2], lane).astype(I32)
        pos_copy(slot).start()

    for static_slot in range(2):
        pl.when((i < n_steps) & (slot == static_slot))(
            functools.partial(route_block, static_slot))

    @pl.when(i == n_steps)
    def _():
        def issue(r, carry):
            for k in range(2):
                row_copy(r, k).start(priority=k)
            return carry

        pos_copy(prev).wait()
        lax.fori_loop(0, tm, issue, 0, unroll=8)
        wait_step_rows()

        for vec, scal in ((cnt_vmem, cnt_smem), (ptab_vmem, ptab_smem)):
            copy = pltpu.make_async_copy(vec, scal, fill_sem)
            copy.start()
            copy.wait()
        _finish_pages(trash_row0, cnt_smem, ptab_smem, xs_ref, tab_ref, zero_scr, fill_sem)


def _out_route(ya, yb, sa, sb, x2, ga, gb, w_out, gffn, wr, br, n_page_rows):
    t = x2.shape[0]
    tm = OUT_ROWS
    n_steps = t // tm
    const = lambda i: (0, 0)
    rows = lambda i: (jnp.minimum(i, n_steps - 1), 0)
    return pl.pallas_call(
        functools.partial(_out_route_body, n_page_rows),
        name="out_route",
        grid=(n_steps + 1,),
        in_specs=[
            pl.BlockSpec((tm, GMLP_WIDTH), rows),
            pl.BlockSpec((tm, CONV_CH), rows),
            pl.BlockSpec((tm, LANES), rows),
            pl.BlockSpec((tm, LANES), rows),
            pl.BlockSpec((tm, D_MODEL), rows),
            pl.BlockSpec((1, GMLP_WIDTH), const),
            pl.BlockSpec((1, CONV_CH), const),
            pl.BlockSpec((D_MODEL, D_MODEL), const, pipeline_mode=pl.Buffered(1)),
            pl.BlockSpec((1, D_MODEL), const),
            pl.BlockSpec((D_MODEL, LANES), const),
            pl.BlockSpec((1, LANES), const),
        ],
        out_specs=[
            pl.BlockSpec((tm, D_MODEL), rows),
            pl.BlockSpec(memory_space=pl.ANY),
            pl.BlockSpec(memory_space=pltpu.SMEM),
        ],
        out_shape=[
            jax.ShapeDtypeStruct((t, D_MODEL), F32),
            jax.ShapeDtypeStruct((n_page_rows + 2 * tm, XS_COLS), F32),
            jax.ShapeDtypeStruct((TAB_ROWS, TAB_LANES), I32),
        ],
        scratch_shapes=[
            pltpu.VMEM((2, tm, XS_COLS), F32),
            pltpu.VMEM((SUBLANES, tm), I32),
            pltpu.SMEM((2, SUBLANES, tm), I32),
            pltpu.VMEM((SUBLANES, LANES), F32),
            pltpu.VMEM((N_EXPERTS, LANES), F32),
            pltpu.VMEM((SUBLANES, LANES), I32),
            pltpu.VMEM((N_EXPERTS, LANES), I32),
            pltpu.SMEM((SUBLANES, LANES), I32),
            pltpu.SMEM((N_EXPERTS, LANES), I32),
            pltpu.VMEM((ZERO_ROWS, XS_COLS), F32),
            pltpu.SemaphoreType.DMA,
            pltpu.SemaphoreType.DMA,
            pltpu.SemaphoreType.DMA,
        ],
        compiler_params=pltpu.CompilerParams(
            dimension_semantics=("arbitrary",),
            vmem_limit_bytes=VMEM_LIMIT),
    )(ya, yb, sa, sb, x2, ga, gb, w_out, gffn, wr, br)


def _zero_range_copies(zero_scr, xs_ref, sem, start, length):
    head = length & (SUBLANES - 1)
    copies = []
    for j in range(SUBLANES - 1):
        copy = pltpu.make_async_copy(
            zero_scr.at[pl.ds(0, 1)], xs_ref.at[pl.ds(start + j, 1)], sem)
        copies.append((j < head, copy))
    body_start = start + head
    body = length - head
    size = ZERO_ROWS
    while size >= SUBLANES:
        offset = pl.multiple_of(body_start + (body & ~(2 * size - 1)), SUBLANES)
        copy = pltpu.make_async_copy(
            zero_scr.at[pl.ds(0, size)], xs_ref.at[pl.ds(offset, size)], sem)
        copies.append(((body & size) != 0, copy))
        size //= 2
    return copies


def _expert_mlp_body(n_tokens, tab_ref, xs_ref, wg_hbm, wu_hbm, wd_hbm, out2_ref,
                     wg_f32, wu_f32, wd_f32, wg_scr, wu_scr, wd_scr, out_buf, dest_vmem,
                     dest_smem, row_sem, dest_sem, w_sem):
    s = pl.program_id(0)
    n_pages = tab_ref[TAB_N_PAGES, 0]
    bm = xs_ref.shape[0]
    cur = lax.rem(s, 3)
    prv = lax.rem(s + 2, 3)
    par = lax.rem(s, 2)
    ppar = 1 - par
    trash_row0 = 2 * n_tokens
    expert = tab_ref[TAB_EXPERT, s]
    run_len = tab_ref[TAB_RUN_LEN, s]
    wslot = tab_ref[TAB_WSLOT, s]

    def weight_copies(e, slot):
        return [
            pltpu.make_async_copy(src.at[e], dst.at[slot], w_sem.at[slot])
            for src, dst in ((wg_hbm, wg_f32), (wu_hbm, wu_f32), (wd_hbm, wd_f32))
        ]

    def row_copy(r, prv=prv):
        return pltpu.make_async_copy(
            out_buf.at[prv, pl.ds(r, 1)], out2_ref.at[pl.ds(dest_smem[ppar, 0, r], 1)],
            row_sem.at[prv])

    def wait_rows(slot):
        pltpu.make_async_copy(out_buf.at[0], out2_ref.at[pl.ds(0, bm)], row_sem.at[slot]).wait()

    def dest_copy(p):
        return pltpu.make_async_copy(dest_vmem, dest_smem.at[p], dest_sem)

    @pl.when(s == 0)
    def _():
        out_buf[2] = jnp.zeros((bm, D_MODEL), F32)
        for half in range(2):
            clear = pltpu.make_async_copy(
                out_buf.at[2], out2_ref.at[pl.ds(trash_row0 + half * bm, bm)], dest_sem)
            clear.start()
            clear.wait()

        def fill(r, carry):
            dest_smem[1, 0, r] = trash_row0 + bm + r
            return carry

        lax.fori_loop(0, bm, fill, 0)
        for copy in weight_copies(expert, wslot):
            copy.start()

    @pl.when((s >= 2) & (s <= n_pages))
    def _():
        wait_rows(cur)

    @pl.when((s >= 1) & (s <= n_pages))
    def _():
        dest_copy(ppar).wait()

    @pl.when((s < n_pages) & (run_len > 0))
    def _():
        for copy in weight_copies(expert, wslot):
            copy.wait()
        wg_scr[...] = wg_f32[wslot].astype(BF16)
        wu_scr[...] = wu_f32[wslot].astype(BF16)
        wd_scr[...] = wd_f32[wslot].astype(BF16)
        next_run = s + run_len

        @pl.when(next_run < n_pages)
        def _():
            for copy in weight_copies(tab_ref[TAB_EXPERT, next_run], 1 - wslot):
                copy.start(priority=1)

    def compute_page(cur):
        prv = (cur + 2) % 3
        down_parts = 2
        n_groups = 2 + down_parts
        group_rows = bm // n_groups
        part_d = D_MODEL // down_parts

        def scatter_group(g):
            for r in range(g * group_rows, (g + 1) * group_rows):
                row_copy(r, prv).start(priority=r % 2)

        x = xs_ref[:, 0:D_MODEL].astype(BF16)
        gate = jnp.dot(x, wg_scr[...], preferred_element_type=F32)

        meta = xs_ref[:, D_MODEL:XS_COLS]
        own_row = (tab_ref[TAB_ORDER, s] * bm
                   + lax.broadcasted_iota(I32, (bm, 1), 0)).astype(F32)
        local = lax.broadcasted_iota(I32, (bm, 1), 0).astype(F32)
        token = meta[:, META_TOKEN:META_TOKEN + 1]
        first = meta[:, META_POS0:META_POS0 + 1] == own_row
        weight = jnp.where(first, meta[:, META_GATE0:META_GATE0 + 1],
                           meta[:, META_GATE1:META_GATE1 + 1])
        plane = jnp.where(first, 0.0, float(n_tokens))
        dest = jnp.where(token > 0.0, plane + token - 1.0,
                         (trash_row0 + par * bm).astype(F32) + local)
        lane = lax.broadcasted_iota(I32, (bm, LANES), 1)
        dest_vmem[...] = _lane_dense([dest], lane).astype(I32)
        dest_copy(par).start()
        scatter_group(0)
        up = jnp.dot(x, wu_scr[...], preferred_element_type=F32)
        scatter_group(1)
        hidden = (jax.nn.silu(gate) * up).astype(BF16)

        for part in range(down_parts):
            cols = slice(part * part_d, (part + 1) * part_d)
            out_buf[cur, :, cols] = jnp.dot(
                hidden, wd_scr[:, cols], preferred_element_type=F32) * weight
            scatter_group(2 + part)

    for static_cur in range(3):
        pl.when((s < n_pages) & (cur == static_cur))(
            functools.partial(compute_page, static_cur))

    @pl.when(s == n_pages)
    def _():
        def issue(pair, carry):
            for k in range(2):
                row_copy(2 * pair + k).start(priority=k)
            return carry

        lax.fori_loop(0, bm // 2, issue, 0, unroll=4)
        wait_rows(lax.rem(s + 1, 3))
        wait_rows(prv)


def _expert_mlp(tab, xs, w_gate, w_up, w_down, n_tokens, n_pages_max):
    bm = EXP_ROWS
    page_of = lambda s, tab: (tab[TAB_ORDER, s], 0)
    return pl.pallas_call(
        functools.partial(_expert_mlp_body, n_tokens),
        name="expert_mlp",
        grid_spec=pltpu.PrefetchScalarGridSpec(
            num_scalar_prefetch=1,
            grid=(n_pages_max + 1,),
            in_specs=[
                pl.BlockSpec((bm, XS_COLS), page_of),
                pl.BlockSpec(memory_space=pl.ANY),
                pl.BlockSpec(memory_space=pl.ANY),
                pl.BlockSpec(memory_space=pl.ANY),
            ],
            out_specs=pl.BlockSpec(memory_space=pl.ANY),
            scratch_shapes=[
                pltpu.VMEM((2, D_MODEL, D_EXPERT), F32),
                pltpu.VMEM((2, D_MODEL, D_EXPERT), F32),
                pltpu.VMEM((2, D_EXPERT, D_MODEL), F32),
                pltpu.VMEM((D_MODEL, D_EXPERT), BF16),
                pltpu.VMEM((D_MODEL, D_EXPERT), BF16),
                pltpu.VMEM((D_EXPERT, D_MODEL), BF16),
                pltpu.VMEM((3, bm, D_MODEL), F32),
                pltpu.VMEM((SUBLANES, bm), I32),
                pltpu.SMEM((2, SUBLANES, bm), I32),
                pltpu.SemaphoreType.DMA((3,)),
                pltpu.SemaphoreType.DMA,
                pltpu.SemaphoreType.DMA((2,)),
            ],
        ),
        out_shape=jax.ShapeDtypeStruct((2 * n_tokens + 2 * bm, D_MODEL), F32),
        compiler_params=pltpu.CompilerParams(
            dimension_semantics=("arbitrary",),
            vmem_limit_bytes=VMEM_LIMIT),
    )(tab, xs, w_gate, w_up, w_down)


def _final_norm_body(h_ref, y0_ref, y1_ref, gfin_ref, out_ref):
    h = h_ref[...] + (y0_ref[...] + y1_ref[...])
    ms = jnp.mean(h * h, axis=-1, keepdims=True)
    out_ref[...] = h * lax.rsqrt(ms + EPS) * gfin_ref[...]


def _final_norm(h, out2, gfin):
    t = h.shape[0]
    tm = FIN_ROWS
    plane1 = t // tm
    return pl.pallas_call(
        _final_norm_body,
        name="final_norm",
        grid=(t // tm,),
        in_specs=[
            pl.BlockSpec((tm, D_MODEL), lambda i: (i, 0)),
            pl.BlockSpec((tm, D_MODEL), lambda i: (i, 0)),
            pl.BlockSpec((tm, D_MODEL), lambda i: (i + plane1, 0)),
            pl.BlockSpec((1, D_MODEL), lambda i: (0, 0)),
        ],
        out_specs=pl.BlockSpec((tm, D_MODEL), lambda i: (i, 0)),
        out_shape=jax.ShapeDtypeStruct((t, D_MODEL), F32),
        compiler_params=pltpu.CompilerParams(
            dimension_semantics=("arbitrary",),
            vmem_limit_bytes=VMEM_LIMIT),
    )(h, out2, out2, gfin)


def kernel(x, norm_mix_g, w_in, gmlp_v_norm_g, gmlp_ws, gmlp_bs, conv_w, out_norm_gmlp_g,
           out_norm_conv_g, w_out, norm_ffn_g, router_group_w, router_group_b, router_expert_w,
           router_expert_b, expert_w_gate, expert_w_up, expert_w_down, norm_final_g):
    batch, seq_len, d_model = x.shape
    t = batch * seq_len
    assert w_in.shape[0] == 1, "single-layer block"
    assert t // EXP_ROWS <= LANES, "page table holds at most LANES pages per expert"
    n_col_groups = GMLP_WIDTH // MIX_COLS
    n_pages_max = 2 * t // EXP_ROWS + N_EXPERTS
    n_page_rows = n_pages_max * EXP_ROWS
    x2 = x.reshape(t, d_model)

    vng = gmlp_v_norm_g[0].reshape(n_col_groups, 1, MIX_COLS)
    bsb = jnp.broadcast_to(gmlp_bs[0][:, :, None], (GMLP_HEADS, GMLP_BLOCK, HEAD_DIM))
    taps = conv_w[0].reshape(CONV_K, n_col_groups, MIX_COLS).transpose(1, 0, 2)
    unused = LANES - N_EXPERTS - N_GROUPS
    wr = jnp.concatenate(
        [router_expert_w[0], router_group_w[0], jnp.zeros((d_model, unused), F32)],
        axis=1).astype(BF16)
    br = jnp.concatenate(
        [router_expert_b[0], router_group_b[0], jnp.zeros((unused,), F32)])[None, :]

    ya, yb, sa, sb = _mixer_proj(x2, norm_mix_g[0][None, :], w_in[0].astype(BF16), vng,
                                 gmlp_ws[0], bsb, taps, seq_len)
    h, xs, tab = _out_route(
        ya, yb, sa, sb, x2, out_norm_gmlp_g[0][None, :], out_norm_conv_g[0][None, :],
        w_out[0].astype(BF16), norm_ffn_g[0][None, :], wr, br, n_page_rows)
    out2 = _expert_mlp(tab, xs, expert_w_gate[0], expert_w_up[0], expert_w_down[0],
                       t, n_pages_max)
    out = _final_norm(h, out2, norm_final_g[None, :])
    return out.reshape(batch, seq_len, d_model)
```

```python
import functools

import jax
import jax.numpy as jnp
from jax import lax
from jax.experimental import pallas as pl
from jax.experimental.pallas import tpu as pltpu

F32 = jnp.float32
BF16 = jnp.bfloat16
I32 = jnp.int32

D_MODEL = 2048
CHUNK = 64
GMLP_WIDTH = 1024
GMLP_HEADS = 8
HEAD_DIM = 128
GMLP_BLOCK = 128
CONV_CH = 1024
CONV_K = 3
N_GROUPS = 4
EXPERTS_PER_GROUP = 8
N_EXPERTS = 32
D_EXPERT = 512
EPS = 1e-6

LANES = 128
SUBLANES = 8

MIX_ROWS = 1024
MIX_COLS = 256
OUT_ROWS = 512
OUT_CHUNK = 512
EXP_ROWS = 256
FIN_ROWS = 512
ZERO_ROWS = 128
GROUP_LANE0 = N_EXPERTS
MASKED = -1e30
VMEM_LIMIT = 56 * 1024 * 1024

XS_COLS = D_MODEL + LANES
META_TOKEN = 0
META_GATE0 = 1
META_POS0 = 2
META_GATE1 = 3
META_POS1 = 4

TAB_ORDER = 0
TAB_EXPERT = 1
TAB_N_PAGES = 2
TAB_RUN_LEN = 3
TAB_WSLOT = 4
TAB_ROWS = 8
TAB_LANES = 256


def _rms_scale(sumsq, width):
    return lax.rsqrt(sumsq / width + EPS)


def _mixer_proj_body(seq_len, x_ref, gmix_ref, wu_ref, wv_ref, wbg_ref, wcg_ref, whv_ref,
                     vng_ref, ws_ref, bsb_ref, cw_ref,
                     ya_ref, yb_ref, sa_ref, sb_ref, xn_scr, carry_scr):
    i = pl.program_id(0)
    c = pl.program_id(1)
    tm = x_ref.shape[0]
    cw = ya_ref.shape[1]
    heads_per_step = cw // HEAD_DIM

    @pl.when(c == 0)
    def _():
        x = x_ref[...]
        ms = jnp.mean(x * x, axis=-1, keepdims=True)
        xn_scr[...] = (x * lax.rsqrt(ms + EPS) * gmix_ref[...]).astype(BF16)
        sa_ref[...] = jnp.zeros_like(sa_ref)
        sb_ref[...] = jnp.zeros_like(sb_ref)

    xn = xn_scr[...]
    project = lambda w_ref: jnp.dot(xn, w_ref[...], preferred_element_type=F32)
    u = jax.nn.gelu(project(wu_ref))
    v = jax.nn.gelu(project(wv_ref))
    bg = project(wbg_ref)
    cg = project(wcg_ref)
    hv = project(whv_ref)

    pos_i = lax.broadcasted_iota(I32, (GMLP_BLOCK, GMLP_BLOCK), 0)
    pos_j = lax.broadcasted_iota(I32, (GMLP_BLOCK, GMLP_BLOCK), 1)
    chunk_shift = CHUNK.bit_length() - 1
    causal = (pos_i >> chunk_shift) >= (pos_j >> chunk_shift)
    vng = vng_ref[0]
    ya_heads = []
    for j in range(heads_per_step):
        head = c * heads_per_step + j
        lanes = slice(j * HEAD_DIM, (j + 1) * HEAD_DIM)
        vj = v[:, lanes]
        ms = jnp.mean(vj * vj, axis=-1, keepdims=True)
        vn = (vj * lax.rsqrt(ms + EPS) * vng[:, lanes]).astype(BF16)
        w_mix = jnp.where(causal, ws_ref[head], 0.0).astype(BF16)
        bias = bsb_ref[head]
        mixed = [
            jnp.dot(w_mix, vn[p * GMLP_BLOCK:(p + 1) * GMLP_BLOCK, :],
                    preferred_element_type=F32) + bias
            for p in range(tm // GMLP_BLOCK)
        ]
        ya_heads.append(u[:, lanes] * jnp.concatenate(mixed, axis=0))
    ya = jnp.concatenate(ya_heads, axis=1) if heads_per_step > 1 else ya_heads[0]

    z = cg * hv
    prev = carry_scr[c]
    seq_start = (i * tm) % seq_len == 0
    prev = jnp.where(seq_start, 0.0, prev)
    row = lax.broadcasted_iota(I32, (tm, cw), 0)
    z1 = jnp.where(row == 0, prev[SUBLANES - 1:SUBLANES, :], pltpu.roll(z, 1, axis=0))
    z2 = jnp.where(row == 0, prev[SUBLANES - 2:SUBLANES - 1, :],
                   jnp.where(row == 1, prev[SUBLANES - 1:SUBLANES, :],
                             pltpu.roll(z, 2, axis=0)))
    taps = cw_ref[0]
    conv = taps[0:1, :] * z2 + taps[1:2, :] * z1 + taps[2:3, :] * z
    yb = bg * conv
    carry_scr[c] = z[tm - SUBLANES:tm, :]

    ya_ref[...] = ya.astype(BF16)
    yb_ref[...] = yb.astype(BF16)
    ya2 = ya * ya
    yb2 = yb * yb
    sa_ref[...] += sum(ya2[:, k * LANES:(k + 1) * LANES] for k in range(cw // LANES))
    sb_ref[...] += sum(yb2[:, k * LANES:(k + 1) * LANES] for k in range(cw // LANES))


def _mixer_proj(x2, gmix, w_in, vng, ws, bsb, conv_taps, seq_len):
    t = x2.shape[0]
    cw = MIX_COLS
    tm = MIX_ROWS
    n_groups = GMLP_WIDTH // cw
    grid = (t // tm, n_groups)
    w_part = lambda k: pl.BlockSpec((D_MODEL, cw), lambda i, c: (0, k * n_groups + c))
    return pl.pallas_call(
        functools.partial(_mixer_proj_body, seq_len),
        name="mixer_proj",
        grid=grid,
        in_specs=[
            pl.BlockSpec((tm, D_MODEL), lambda i, c: (i, 0)),
            pl.BlockSpec((1, D_MODEL), lambda i, c: (0, 0)),
            w_part(0), w_part(1), w_part(2), w_part(3), w_part(4),
            pl.BlockSpec((1, 1, cw), lambda i, c: (c, 0, 0)),
            pl.BlockSpec((GMLP_HEADS, GMLP_BLOCK, GMLP_BLOCK), lambda i, c: (0, 0, 0)),
            pl.BlockSpec((GMLP_HEADS, GMLP_BLOCK, HEAD_DIM), lambda i, c: (0, 0, 0)),
            pl.BlockSpec((1, CONV_K, cw), lambda i, c: (c, 0, 0)),
        ],
        out_specs=[
            pl.BlockSpec((tm, cw), lambda i, c: (i, c)),
            pl.BlockSpec((tm, cw), lambda i, c: (i, c)),
            pl.BlockSpec((tm, LANES), lambda i, c: (i, 0)),
            pl.BlockSpec((tm, LANES), lambda i, c: (i, 0)),
        ],
        out_shape=[
            jax.ShapeDtypeStruct((t, GMLP_WIDTH), BF16),
            jax.ShapeDtypeStruct((t, CONV_CH), BF16),
            jax.ShapeDtypeStruct((t, LANES), F32),
            jax.ShapeDtypeStruct((t, LANES), F32),
        ],
        scratch_shapes=[
            pltpu.VMEM((tm, D_MODEL), BF16),
            pltpu.VMEM((n_groups, SUBLANES, cw), F32),
        ],
        compiler_params=pltpu.CompilerParams(
            dimension_semantics=("arbitrary", "arbitrary"),
            vmem_limit_bytes=VMEM_LIMIT),
    )(x2, gmix, w_in, w_in, w_in, w_in, w_in, vng, ws, bsb, conv_taps)


def _lane_cumsum(v, lane):
    shift = 1
    while shift < N_EXPERTS:
        v = v + jnp.where(lane >= shift, pltpu.roll(v, shift, axis=1), 0.0)
        shift *= 2
    return v


def _lane_dense(col_values, lane):
    tile = jnp.zeros(lane.shape, F32)
    for k, col in enumerate(col_values):
        tile = jnp.where(lane == k, col, tile)
    return tile.T[0:SUBLANES, :]


def _finish_pages(n_page_rows, cnt_ref, ptab_ref, xs_ref, tab_ref, zero_scr, sem):
    page_shift = EXP_ROWS.bit_length() - 1
    n_slots = tab_ref.shape[1]
    zero_scr[...] = jnp.zeros_like(zero_scr)

    def clear(s, carry):
        for row in range(TAB_ROWS):
            tab_ref[row, s] = 0
        return carry

    lax.fori_loop(0, n_slots, clear, 0)

    def pages_of(e):
        return (cnt_ref[0, e] + (EXP_ROWS - 1)) >> page_shift

    def list_pages(e, carry):
        first_slot, run = carry

        def put(j, carry):
            tab_ref[TAB_ORDER, first_slot + j] = ptab_ref[e, j]
            tab_ref[TAB_EXPERT, first_slot + j] = e
            tab_ref[TAB_WSLOT, first_slot + j] = run & 1
            return carry

        n_pages_e = pages_of(e)
        lax.fori_loop(0, n_pages_e, put, 0)

        @pl.when(n_pages_e > 0)
        def _():
            tab_ref[TAB_RUN_LEN, first_slot] = n_pages_e

        return first_slot + n_pages_e, run + jnp.where(n_pages_e > 0, 1, 0)

    n_pages, _ = lax.fori_loop(0, N_EXPERTS, list_pages, (0, 0))
    tab_ref[TAB_N_PAGES, 0] = n_pages
    last_page = tab_ref[TAB_ORDER, n_pages - 1]
    last_expert = tab_ref[TAB_EXPERT, n_pages - 1]

    def repeat_last(s, carry):
        tab_ref[TAB_ORDER, s] = last_page
        tab_ref[TAB_EXPERT, s] = last_expert
        return carry

    lax.fori_loop(n_pages, n_slots, repeat_last, 0)

    def pad_copies(e):
        used = cnt_ref[0, e] & (EXP_ROWS - 1)
        last = ptab_ref[e, jnp.maximum(pages_of(e) - 1, 0)]
        return _zero_range_copies(zero_scr, xs_ref, sem, last * EXP_ROWS + used,
                                  (EXP_ROWS - used) & (EXP_ROWS - 1))

    def tail_copies(p):
        return [
            pltpu.make_async_copy(
                zero_scr,
                xs_ref.at[pl.ds(pl.multiple_of(p * EXP_ROWS + part * ZERO_ROWS, ZERO_ROWS),
                                ZERO_ROWS)],
                sem)
            for part in range(EXP_ROWS // ZERO_ROWS)
        ]

    def start_pads(e, carry):
        for needed, copy in pad_copies(e):
            pl.when(needed)(copy.start)
        return carry

    def wait_pads(e, carry):
        for needed, copy in pad_copies(e):
            pl.when(needed)(copy.wait)
        return carry

    def start_tail(p, carry):
        for copy in tail_copies(p):
            copy.start()
        return carry

    def wait_tail(p, carry):
        for copy in tail_copies(p):
            copy.wait()
        return carry

    total_pages = n_page_rows // EXP_ROWS
    lax.fori_loop(0, N_EXPERTS, start_pads, 0)
    lax.fori_loop(n_pages, total_pages, start_tail, 0)
    lax.fori_loop(0, N_EXPERTS, wait_pads, 0)
    lax.fori_loop(n_pages, total_pages, wait_tail, 0)


def _out_route_body(trash_row0, ya_ref, yb_ref, sa_ref, sb_ref, x_ref, ga_ref, gb_ref, wout_ref,
                    gffn_ref, wr_ref, br_ref, h_ref, xs_ref, tab_ref,
                    row_buf, pos_vmem, pos_smem, st_scr, ptab_scr, cnt_vmem, ptab_vmem,
                    cnt_smem, ptab_smem, zero_scr, row_sem, pos_sem, fill_sem):
    i = pl.program_id(0)
    n_steps = pl.num_programs(0) - 1
    tm = x_ref.shape[0]
    slot = lax.rem(i, 2)
    prev = 1 - slot

    def row_copy(r, k, prev=prev):
        return pltpu.make_async_copy(
            row_buf.at[prev, pl.ds(r, 1)], xs_ref.at[pl.ds(pos_smem[prev, k, r], 1)], row_sem)

    def wait_step_rows():
        for _ in range(2):
            pltpu.make_async_copy(row_buf.at[0], xs_ref.at[pl.ds(0, tm)], row_sem).wait()

    def pos_copy(s):
        return pltpu.make_async_copy(pos_vmem, pos_smem.at[s], pos_sem)

    @pl.when(i == 0)
    def _():
        st_scr[...] = jnp.zeros_like(st_scr)
        ptab_scr[...] = jnp.zeros_like(ptab_scr)
        row_buf[1] = jnp.zeros((tm, XS_COLS), F32)
        which = lax.broadcasted_iota(I32, (SUBLANES, tm), 0)
        pos_vmem[...] = (trash_row0 + lax.broadcasted_iota(I32, (SUBLANES, tm), 1)
                         + jnp.where(which == 1, tm, 0))
        pos_copy(1).start()

    @pl.when(i > 0)
    def _():
        wait_step_rows()

    def route_block(slot):
        prev = 1 - slot
        rows_per_chunk = tm // (D_MODEL // OUT_CHUNK)

        ra = _rms_scale(jnp.sum(sa_ref[...], axis=-1, keepdims=True), GMLP_WIDTH)
        rb = _rms_scale(jnp.sum(sb_ref[...], axis=-1, keepdims=True), CONV_CH)
        yna = (ya_ref[...].astype(F32) * ra * ga_ref[...]).astype(BF16)
        ynb = (yb_ref[...].astype(F32) * rb * gb_ref[...]).astype(BF16)
        yn = jnp.concatenate([yna, ynb], axis=1)
        for j in range(D_MODEL // OUT_CHUNK):
            cols = slice(j * OUT_CHUNK, (j + 1) * OUT_CHUNK)
            h_ref[:, cols] = x_ref[:, cols] + jnp.dot(
                yn, wout_ref[:, cols], preferred_element_type=F32)
            if j == 0:
                pos_copy(prev).wait()
            for r in range(j * rows_per_chunk, (j + 1) * rows_per_chunk):
                for k in range(2):
                    row_copy(r, k, prev).start(priority=k)
        h = h_ref[...]

        ms = jnp.mean(h * h, axis=-1, keepdims=True)
        hn = h * lax.rsqrt(ms + EPS) * gffn_ref[...]
        logits = jnp.dot(hn.astype(BF16), wr_ref[...], preferred_element_type=F32) + br_ref[...]

        lane = lax.broadcasted_iota(I32, (tm, LANES), 1)
        lane_f = lane.astype(F32)
        big = float(LANES)

        is_g = (lane >= GROUP_LANE0) & (lane < GROUP_LANE0 + N_GROUPS)
        lg = jnp.where(is_g, logits, MASKED)
        lg_top = jnp.max(lg, axis=-1, keepdims=True)
        g_idx = jnp.min(jnp.where(is_g & (lg == lg_top), lane_f - GROUP_LANE0, big),
                        axis=-1, keepdims=True).astype(I32)
        pg_top = 1.0 / jnp.sum(jnp.where(is_g, jnp.exp(lg - lg_top), 0.0), axis=-1, keepdims=True)

        group_shift = EXPERTS_PER_GROUP.bit_length() - 1
        in_grp = (lane < N_EXPERTS) & ((lane >> group_shift) == g_idx)
        le = jnp.where(in_grp, logits, MASKED)
        le1 = jnp.max(le, axis=-1, keepdims=True)
        e1 = jnp.min(jnp.where(in_grp & (le == le1), lane_f, big), axis=-1, keepdims=True)
        denom = jnp.sum(jnp.where(in_grp, jnp.exp(le - le1), 0.0), axis=-1, keepdims=True)
        rest = in_grp & (lane_f != e1)
        le2 = jnp.max(jnp.where(rest, le, MASKED), axis=-1, keepdims=True)
        e2 = jnp.min(jnp.where(rest & (le == le2), lane_f, big), axis=-1, keepdims=True)
        q1 = 1.0 / denom
        q2 = jnp.exp(le2 - le1) / denom
        qs = q1 + q2
        gate1 = pg_top * (q1 / qs)
        gate2 = pg_top * (q2 / qs)

        is1 = lane_f == e1
        is2 = lane_f == e2
        sel = jnp.where(is1 | is2, 1.0, 0.0)
        t_i = lax.broadcasted_iota(I32, (tm, tm), 0)
        t_j = lax.broadcasted_iota(I32, (tm, tm), 1)
        before = jnp.where(t_j < t_i, 1.0, 0.0).astype(BF16)
        run0 = st_scr[0:1, :]
        cur_page = st_scr[1:2, :]
        next_free = st_scr[2:3, :]
        cum = jnp.dot(before, sel.astype(BF16), preferred_element_type=F32) + run0

        lane_row = lax.broadcasted_iota(I32, (1, LANES), 1)
        run1 = run0 + jnp.sum(sel, axis=0, keepdims=True)
        pages0 = jnp.ceil(run0 / EXP_ROWS)
        n_new = jnp.ceil(run1 / EXP_ROWS) - pages0
        first_new = next_free + _lane_cumsum(n_new, lane_row) - n_new

        def pick(mask, per_expert):
            return jnp.sum(jnp.where(mask, per_expert, 0.0), axis=-1, keepdims=True)

        def place(mask):
            rank = pick(mask, cum)
            page_idx = jnp.floor(rank / EXP_ROWS)
            owned = pick(mask, pages0)
            page = jnp.where(page_idx < owned, pick(mask, cur_page),
                             pick(mask, first_new) + (page_idx - owned))
            return page * EXP_ROWS + (rank - page_idx * EXP_ROWS)

        pos1 = place(is1)
        pos2 = place(is2)

        st_scr[0:1, :] = run1
        st_scr[1:2, :] = jnp.where(n_new > 0, first_new + n_new - 1, cur_page)
        st_scr[2:3, :] = next_free + jnp.sum(n_new, axis=-1, keepdims=True)

        per_expert = jnp.concatenate(
            [pages0, n_new, first_new, jnp.zeros((LANES - 3, LANES), F32)], axis=0).T
        owned_c = per_expert[0:N_EXPERTS, 0:1]
        n_new_c = per_expert[0:N_EXPERTS, 1:2]
        first_c = per_expert[0:N_EXPERTS, 2:3]
        idx = lax.broadcasted_iota(I32, (N_EXPERTS, LANES), 1).astype(F32)
        fresh = (idx >= owned_c) & (idx < owned_c + n_new_c)
        ptab_scr[...] = jnp.where(fresh, first_c + (idx - owned_c), ptab_scr[...])
        cnt_vmem[...] = jnp.broadcast_to(run1, cnt_vmem.shape).astype(I32)
        ptab_vmem[...] = ptab_scr[...].astype(I32)

        token = (i * tm + lax.broadcasted_iota(I32, (tm, 1), 0) + 1).astype(F32)
        meta = jnp.zeros((tm, LANES), F32)
        for k, col in ((META_TOKEN, token), (META_GATE0, gate1), (META_POS0, pos1),
                       (META_GATE1, gate2), (META_POS1, pos2)):
            meta = jnp.where(lane == k, col, meta)
        row_buf[slot, :, 0:D_MODEL] = hn
        row_buf[slot, :, D_MODEL:XS_COLS] = meta
        pos_vmem[...] = _lane_dense([pos1, pos2], lane).astype(I32)
        pos_copy(slot).start()

    for static_slot in range(2):
        pl.when((i < n_steps) & (slot == static_slot))(
            functools.partial(route_block, static_slot))

    @pl.when(i == n_steps)
    def _():
        def issue(r, carry):
            for k in range(2):
                row_copy(r, k).start(priority=k)
            return carry

        pos_copy(prev).wait()
        lax.fori_loop(0, tm, issue, 0, unroll=8)
        wait_step_rows()

        for vec, scal in ((cnt_vmem, cnt_smem), (ptab_vmem, ptab_smem)):
            copy = pltpu.make_async_copy(vec, scal, fill_sem)
            copy.start()
            copy.wait()
        _finish_pages(trash_row0, cnt_smem, ptab_smem, xs_ref, tab_ref, zero_scr, fill_sem)


def _out_route(ya, yb, sa, sb, x2, ga, gb, w_out, gffn, wr, br, n_page_rows):
    t = x2.shape[0]
    tm = OUT_ROWS
    n_steps = t // tm
    const = lambda i: (0, 0)
    rows = lambda i: (jnp.minimum(i, n_steps - 1), 0)
    return pl.pallas_call(
        functools.partial(_out_route_body, n_page_rows),
        name="out_route",
        grid=(n_steps + 1,),
        in_specs=[
            pl.BlockSpec((tm, GMLP_WIDTH), rows),
            pl.BlockSpec((tm, CONV_CH), rows),
            pl.BlockSpec((tm, LANES), rows),
            pl.BlockSpec((tm, LANES), rows),
            pl.BlockSpec((tm, D_MODEL), rows),
            pl.BlockSpec((1, GMLP_WIDTH), const),
            pl.BlockSpec((1, CONV_CH), const),
            pl.BlockSpec((D_MODEL, D_MODEL), const, pipeline_mode=pl.Buffered(1)),
            pl.BlockSpec((1, D_MODEL), const),
            pl.BlockSpec((D_MODEL, LANES), const),
            pl.BlockSpec((1, LANES), const),
        ],
        out_specs=[
            pl.BlockSpec((tm, D_MODEL), rows),
            pl.BlockSpec(memory_space=pl.ANY),
            pl.BlockSpec(memory_space=pltpu.SMEM),
        ],
        out_shape=[
            jax.ShapeDtypeStruct((t, D_MODEL), F32),
            jax.ShapeDtypeStruct((n_page_rows + 2 * tm, XS_COLS), F32),
            jax.ShapeDtypeStruct((TAB_ROWS, TAB_LANES), I32),
        ],
        scratch_shapes=[
            pltpu.VMEM((2, tm, XS_COLS), F32),
            pltpu.VMEM((SUBLANES, tm), I32),
            pltpu.SMEM((2, SUBLANES, tm), I32),
            pltpu.VMEM((SUBLANES, LANES), F32),
            pltpu.VMEM((N_EXPERTS, LANES), F32),
            pltpu.VMEM((SUBLANES, LANES), I32),
            pltpu.VMEM((N_EXPERTS, LANES), I32),
            pltpu.SMEM((SUBLANES, LANES), I32),
            pltpu.SMEM((N_EXPERTS, LANES), I32),
            pltpu.VMEM((ZERO_ROWS, XS_COLS), F32),
            pltpu.SemaphoreType.DMA,
            pltpu.SemaphoreType.DMA,
            pltpu.SemaphoreType.DMA,
        ],
        compiler_params=pltpu.CompilerParams(
            dimension_semantics=("arbitrary",),
            vmem_limit_bytes=VMEM_LIMIT),
    )(ya, yb, sa, sb, x2, ga, gb, w_out, gffn, wr, br)


def _zero_range_copies(zero_scr, xs_ref, sem, start, length):
    head = length & (SUBLANES - 1)
    copies = []
    for j in range(SUBLANES - 1):
        copy = pltpu.make_async_copy(
            zero_scr.at[pl.ds(0, 1)], xs_ref.at[pl.ds(start + j, 1)], sem)
        copies.append((j < head, copy))
    body_start = start + head
    body = length - head
    size = ZERO_ROWS
    while size >= SUBLANES:
        offset = pl.multiple_of(body_start + (body & ~(2 * size - 1)), SUBLANES)
        copy = pltpu.make_async_copy(
            zero_scr.at[pl.ds(0, size)], xs_ref.at[pl.ds(offset, size)], sem)
        copies.append(((body & size) != 0, copy))
        size //= 2
    return copies


SLOTS_PER_STEP = 2


def _expert_mlp_body(n_tokens, tab_ref, *refs):
    page_refs, rest = refs[:SLOTS_PER_STEP], refs[SLOTS_PER_STEP:]
    for j, xs_ref in enumerate(page_refs):
        _expert_slot(n_tokens, SLOTS_PER_STEP * pl.program_id(0) + j, tab_ref, xs_ref, *rest)


def _expert_slot(n_tokens, s, tab_ref, xs_ref, wg_hbm, wu_hbm, wd_hbm, out2_ref,
                 wg_f32, wu_f32, wd_f32, wg_scr, wu_scr, wd_scr, out_buf, dest_vmem,
                 dest_smem, row_sem, dest_sem, w_sem):
    n_pages = tab_ref[TAB_N_PAGES, 0]
    bm = xs_ref.shape[0]
    cur = lax.rem(s, 3)
    prv = lax.rem(s + 2, 3)
    par = lax.rem(s, 2)
    ppar = 1 - par
    trash_row0 = 2 * n_tokens
    expert = tab_ref[TAB_EXPERT, s]
    run_len = tab_ref[TAB_RUN_LEN, s]
    wslot = tab_ref[TAB_WSLOT, s]

    def weight_copies(e, slot):
        return [
            pltpu.make_async_copy(src.at[e], dst.at[slot], w_sem.at[slot])
            for src, dst in ((wg_hbm, wg_f32), (wu_hbm, wu_f32), (wd_hbm, wd_f32))
        ]

    def row_copy(r, prv=prv):
        return pltpu.make_async_copy(
            out_buf.at[prv, pl.ds(r, 1)], out2_ref.at[pl.ds(dest_smem[ppar, 0, r], 1)],
            row_sem.at[prv])

    def wait_rows(slot):
        pltpu.make_async_copy(out_buf.at[0], out2_ref.at[pl.ds(0, bm)], row_sem.at[slot]).wait()

    def dest_copy(p):
        return pltpu.make_async_copy(dest_vmem, dest_smem.at[p], dest_sem)

    @pl.when(s == 0)
    def _():
        out_buf[2] = jnp.zeros((bm, D_MODEL), F32)
        for half in range(2):
            clear = pltpu.make_async_copy(
                out_buf.at[2], out2_ref.at[pl.ds(trash_row0 + half * bm, bm)], dest_sem)
            clear.start()
            clear.wait()

        def fill(r, carry):
            dest_smem[1, 0, r] = trash_row0 + bm + r
            return carry

        lax.fori_loop(0, bm, fill, 0)
        for copy in weight_copies(expert, wslot):
            copy.start()

    @pl.when((s >= 2) & (s <= n_pages))
    def _():
        wait_rows(cur)

    @pl.when((s >= 1) & (s <= n_pages))
    def _():
        dest_copy(ppar).wait()

    @pl.when((s < n_pages) & (run_len > 0))
    def _():
        for copy in weight_copies(expert, wslot):
            copy.wait()
        wg_scr[...] = wg_f32[wslot].astype(BF16)
        wu_scr[...] = wu_f32[wslot].astype(BF16)
        wd_scr[...] = wd_f32[wslot].astype(BF16)
        next_run = s + run_len

        @pl.when(next_run < n_pages)
        def _():
            for copy in weight_copies(tab_ref[TAB_EXPERT, next_run], 1 - wslot):
                copy.start()

    def compute_page(cur):
        prv = (cur + 2) % 3
        down_parts = 2
        n_groups = 2 + down_parts
        group_rows = bm // n_groups
        part_d = D_MODEL // down_parts

        def scatter_group(g):
            for r in range(g * group_rows, (g + 1) * group_rows):
                row_copy(r, prv).start(priority=r % 2)

        x = xs_ref[:, 0:D_MODEL].astype(BF16)
        gate = jnp.dot(x, wg_scr[...], preferred_element_type=F32)

        meta = xs_ref[:, D_MODEL:XS_COLS]
        own_row = (tab_ref[TAB_ORDER, s] * bm
                   + lax.broadcasted_iota(I32, (bm, 1), 0)).astype(F32)
        local = lax.broadcasted_iota(I32, (bm, 1), 0).astype(F32)
        token = meta[:, META_TOKEN:META_TOKEN + 1]
        first = meta[:, META_POS0:META_POS0 + 1] == own_row
        weight = jnp.where(first, meta[:, META_GATE0:META_GATE0 + 1],
                           meta[:, META_GATE1:META_GATE1 + 1])
        plane = jnp.where(first, 0.0, float(n_tokens))
        dest = jnp.where(token > 0.0, plane + token - 1.0,
                         (trash_row0 + par * bm).astype(F32) + local)
        lane = lax.broadcasted_iota(I32, (bm, LANES), 1)
        dest_vmem[...] = _lane_dense([dest], lane).astype(I32)
        dest_copy(par).start()
        scatter_group(0)
        up = jnp.dot(x, wu_scr[...], preferred_element_type=F32)
        scatter_group(1)
        hidden = (jax.nn.silu(gate) * up).astype(BF16)

        for part in range(down_parts):
            cols = slice(part * part_d, (part + 1) * part_d)
            out_buf[cur, :, cols] = jnp.dot(
                hidden, wd_scr[:, cols], preferred_element_type=F32) * weight
            scatter_group(2 + part)

    for static_cur in range(3):
        pl.when((s < n_pages) & (cur == static_cur))(
            functools.partial(compute_page, static_cur))

    @pl.when(s == n_pages)
    def _():
        def issue(pair, carry):
            for k in range(2):
                row_copy(2 * pair + k).start(priority=k)
            return carry

        lax.fori_loop(0, bm // 2, issue, 0, unroll=4)
        wait_rows(lax.rem(s + 1, 3))
        wait_rows(prv)


def _expert_mlp(tab, xs, w_gate, w_up, w_down, n_tokens, n_pages_max):
    bm = EXP_ROWS
    page_of = lambda j: pl.BlockSpec(
        (bm, XS_COLS), lambda g, tab: (tab[TAB_ORDER, SLOTS_PER_STEP * g + j], 0))
    n_slots = n_pages_max + 1
    assert n_slots + SLOTS_PER_STEP <= TAB_LANES
    return pl.pallas_call(
        functools.partial(_expert_mlp_body, n_tokens),
        name="expert_mlp",
        grid_spec=pltpu.PrefetchScalarGridSpec(
            num_scalar_prefetch=1,
            grid=(pl.cdiv(n_slots, SLOTS_PER_STEP),),
            in_specs=[
                *[page_of(j) for j in range(SLOTS_PER_STEP)],
                pl.BlockSpec(memory_space=pl.ANY),
                pl.BlockSpec(memory_space=pl.ANY),
                pl.BlockSpec(memory_space=pl.ANY),
            ],
            out_specs=pl.BlockSpec(memory_space=pl.ANY),
            scratch_shapes=[
                pltpu.VMEM((2, D_MODEL, D_EXPERT), F32),
                pltpu.VMEM((2, D_MODEL, D_EXPERT), F32),
                pltpu.VMEM((2, D_EXPERT, D_MODEL), F32),
                pltpu.VMEM((D_MODEL, D_EXPERT), BF16),
                pltpu.VMEM((D_MODEL, D_EXPERT), BF16),
                pltpu.VMEM((D_EXPERT, D_MODEL), BF16),
                pltpu.VMEM((3, bm, D_MODEL), F32),
                pltpu.VMEM((SUBLANES, bm), I32),
                pltpu.SMEM((2, SUBLANES, bm), I32),
                pltpu.SemaphoreType.DMA((3,)),
                pltpu.SemaphoreType.DMA,
                pltpu.SemaphoreType.DMA((2,)),
            ],
        ),
        out_shape=jax.ShapeDtypeStruct((2 * n_tokens + 2 * bm, D_MODEL), F32),
        compiler_params=pltpu.CompilerParams(
            dimension_semantics=("arbitrary",),
            vmem_limit_bytes=VMEM_LIMIT),
    )(tab, *([xs] * SLOTS_PER_STEP), w_gate, w_up, w_down)


def _final_norm_body(h_ref, y0_ref, y1_ref, gfin_ref, out_ref):
    h = h_ref[...] + (y0_ref[...] + y1_ref[...])
    ms = jnp.mean(h * h, axis=-1, keepdims=True)
    out_ref[...] = h * lax.rsqrt(ms + EPS) * gfin_ref[...]


def _final_norm(h, out2, gfin):
    t = h.shape[0]
    tm = FIN_ROWS
    plane1 = t // tm
    return pl.pallas_call(
        _final_norm_body,
        name="final_norm",
        grid=(t // tm,),
        in_specs=[
            pl.BlockSpec((tm, D_MODEL), lambda i: (i, 0)),
            pl.BlockSpec((tm, D_MODEL), lambda i: (i, 0)),
            pl.BlockSpec((tm, D_MODEL), lambda i: (i + plane1, 0)),
            pl.BlockSpec((1, D_MODEL), lambda i: (0, 0)),
        ],
        out_specs=pl.BlockSpec((tm, D_MODEL), lambda i: (i, 0)),
        out_shape=jax.ShapeDtypeStruct((t, D_MODEL), F32),
        compiler_params=pltpu.CompilerParams(
            dimension_semantics=("arbitrary",),
            vmem_limit_bytes=VMEM_LIMIT),
    )(h, out2, out2, gfin)


def kernel(x, norm_mix_g, w_in, gmlp_v_norm_g, gmlp_ws, gmlp_bs, conv_w, out_norm_gmlp_g,
           out_norm_conv_g, w_out, norm_ffn_g, router_group_w, router_group_b, router_expert_w,
           router_expert_b, expert_w_gate, expert_w_up, expert_w_down, norm_final_g):
    batch, seq_len, d_model = x.shape
    t = batch * seq_len
    assert w_in.shape[0] == 1, "single-layer block"
    assert t // EXP_ROWS <= LANES, "page table holds at most LANES pages per expert"
    n_col_groups = GMLP_WIDTH // MIX_COLS
    n_pages_max = 2 * t // EXP_ROWS + N_EXPERTS
    n_page_rows = n_pages_max * EXP_ROWS
    x2 = x.reshape(t, d_model)

    vng = gmlp_v_norm_g[0].reshape(n_col_groups, 1, MIX_COLS)
    bsb = jnp.broadcast_to(gmlp_bs[0][:, :, None], (GMLP_HEADS, GMLP_BLOCK, HEAD_DIM))
    taps = conv_w[0].reshape(CONV_K, n_col_groups, MIX_COLS).transpose(1, 0, 2)
    unused = LANES - N_EXPERTS - N_GROUPS
    wr = jnp.concatenate(
        [router_expert_w[0], router_group_w[0], jnp.zeros((d_model, unused), F32)],
        axis=1).astype(BF16)
    br = jnp.concatenate(
        [router_expert_b[0], router_group_b[0], jnp.zeros((unused,), F32)])[None, :]

    ya, yb, sa, sb = _mixer_proj(x2, norm_mix_g[0][None, :], w_in[0].astype(BF16), vng,
                                 gmlp_ws[0], bsb, taps, seq_len)
    h, xs, tab = _out_route(
        ya, yb, sa, sb, x2, out_norm_gmlp_g[0][None, :], out_norm_conv_g[0][None, :],
        w_out[0].astype(BF16), norm_ffn_g[0][None, :], wr, br, n_page_rows)
    out2 = _expert_mlp(tab, xs, expert_w_gate[0], expert_w_up[0], expert_w_down[0],
                       t, n_pages_max)
    out = _final_norm(h, out2, norm_final_g[None, :])
    return out.reshape(batch, seq_len, d_model)
```

```python
import functools

import jax
import jax.numpy as jnp
from jax import lax
from jax.experimental import pallas as pl
from jax.experimental.pallas import tpu as pltpu

F32 = jnp.float32
BF16 = jnp.bfloat16
I32 = jnp.int32

D_MODEL = 2048
CHUNK = 64
GMLP_WIDTH = 1024
GMLP_HEADS = 8
HEAD_DIM = 128
GMLP_BLOCK = 128
CONV_CH = 1024
CONV_K = 3
N_GROUPS = 4
EXPERTS_PER_GROUP = 8
N_EXPERTS = 32
D_EXPERT = 512
EPS = 1e-6

LANES = 128
SUBLANES = 8

MIX_ROWS = 1024
MIX_COLS = 256
OUT_ROWS = 512
OUT_CHUNK = 512
EXP_ROWS = 256
FIN_ROWS = 512
ZERO_ROWS = 128
GROUP_LANE0 = N_EXPERTS
MASKED = -1e30
VMEM_LIMIT = 56 * 1024 * 1024

XS_COLS = D_MODEL + LANES
META_TOKEN = 0
META_GATE0 = 1
META_POS0 = 2
META_GATE1 = 3
META_POS1 = 4

TAB_ORDER = 0
TAB_EXPERT = 1
TAB_N_PAGES = 2
TAB_RUN_LEN = 3
TAB_WSLOT = 4
TAB_ROWS = 8
TAB_LANES = 256


def _rms_scale(sumsq, width):
    return lax.rsqrt(sumsq / width + EPS)


def _mixer_proj_body(seq_len, x_ref, gmix_ref, wu_ref, wv_ref, wbg_ref, wcg_ref, whv_ref,
                     vng_ref, ws_ref, bsb_ref, cw_ref,
                     ya_ref, yb_ref, sa_ref, sb_ref, xn_scr, carry_scr):
    i = pl.program_id(0)
    c = pl.program_id(1)
    tm = x_ref.shape[0]
    cw = ya_ref.shape[1]
    heads_per_step = cw // HEAD_DIM

    @pl.when(c == 0)
    def _():
        x = x_ref[...]
        ms = jnp.mean(x * x, axis=-1, keepdims=True)
        xn_scr[...] = (x * lax.rsqrt(ms + EPS) * gmix_ref[...]).astype(BF16)
        sa_ref[...] = jnp.zeros_like(sa_ref)
        sb_ref[...] = jnp.zeros_like(sb_ref)

    xn = xn_scr[...]
    project = lambda w_ref: jnp.dot(xn, w_ref[...], preferred_element_type=F32)
    u = jax.nn.gelu(project(wu_ref))
    v = jax.nn.gelu(project(wv_ref))
    bg = project(wbg_ref)
    cg = project(wcg_ref)
    hv = project(whv_ref)

    pos_i = lax.broadcasted_iota(I32, (GMLP_BLOCK, GMLP_BLOCK), 0)
    pos_j = lax.broadcasted_iota(I32, (GMLP_BLOCK, GMLP_BLOCK), 1)
    chunk_shift = CHUNK.bit_length() - 1
    causal = (pos_i >> chunk_shift) >= (pos_j >> chunk_shift)
    vng = vng_ref[0]
    ya_heads = []
    for j in range(heads_per_step):
        head = c * heads_per_step + j
        lanes = slice(j * HEAD_DIM, (j + 1) * HEAD_DIM)
        vj = v[:, lanes]
        ms = jnp.mean(vj * vj, axis=-1, keepdims=True)
        vn = (vj * lax.rsqrt(ms + EPS) * vng[:, lanes]).astype(BF16)
        w_mix = jnp.where(causal, ws_ref[head], 0.0).astype(BF16)
        bias = bsb_ref[head]
        mixed = [
            jnp.dot(w_mix, vn[p * GMLP_BLOCK:(p + 1) * GMLP_BLOCK, :],
                    preferred_element_type=F32) + bias
            for p in range(tm // GMLP_BLOCK)
        ]
        ya_heads.append(u[:, lanes] * jnp.concatenate(mixed, axis=0))
    ya = jnp.concatenate(ya_heads, axis=1) if heads_per_step > 1 else ya_heads[0]

    z = cg * hv
    prev = carry_scr[c]
    seq_start = (i * tm) % seq_len == 0
    prev = jnp.where(seq_start, 0.0, prev)
    row = lax.broadcasted_iota(I32, (tm, cw), 0)
    z1 = jnp.where(row == 0, prev[SUBLANES - 1:SUBLANES, :], pltpu.roll(z, 1, axis=0))
    z2 = jnp.where(row == 0, prev[SUBLANES - 2:SUBLANES - 1, :],
                   jnp.where(row == 1, prev[SUBLANES - 1:SUBLANES, :],
                             pltpu.roll(z, 2, axis=0)))
    taps = cw_ref[0]
    conv = taps[0:1, :] * z2 + taps[1:2, :] * z1 + taps[2:3, :] * z
    yb = bg * conv
    carry_scr[c] = z[tm - SUBLANES:tm, :]

    ya_ref[...] = ya.astype(BF16)
    yb_ref[...] = yb.astype(BF16)
    ya2 = ya * ya
    yb2 = yb * yb
    sa_ref[...] += sum(ya2[:, k * LANES:(k + 1) * LANES] for k in range(cw // LANES))
    sb_ref[...] += sum(yb2[:, k * LANES:(k + 1) * LANES] for k in range(cw // LANES))


def _mixer_proj(x2, gmix, w_in, vng, ws, bsb, conv_taps, seq_len):
    t = x2.shape[0]
    cw = MIX_COLS
    tm = MIX_ROWS
    n_groups = GMLP_WIDTH // cw
    grid = (t // tm, n_groups)
    w_part = lambda k: pl.BlockSpec((D_MODEL, cw), lambda i, c: (0, k * n_groups + c))
    return pl.pallas_call(
        functools.partial(_mixer_proj_body, seq_len),
        name="mixer_proj",
        grid=grid,
        in_specs=[
            pl.BlockSpec((tm, D_MODEL), lambda i, c: (i, 0)),
            pl.BlockSpec((1, D_MODEL), lambda i, c: (0, 0)),
            w_part(0), w_part(1), w_part(2), w_part(3), w_part(4),
            pl.BlockSpec((1, 1, cw), lambda i, c: (c, 0, 0)),
            pl.BlockSpec((GMLP_HEADS, GMLP_BLOCK, GMLP_BLOCK), lambda i, c: (0, 0, 0)),
            pl.BlockSpec((GMLP_HEADS, GMLP_BLOCK, HEAD_DIM), lambda i, c: (0, 0, 0)),
            pl.BlockSpec((1, CONV_K, cw), lambda i, c: (c, 0, 0)),
        ],
        out_specs=[
            pl.BlockSpec((tm, cw), lambda i, c: (i, c)),
            pl.BlockSpec((tm, cw), lambda i, c: (i, c)),
            pl.BlockSpec((tm, LANES), lambda i, c: (i, 0)),
            pl.BlockSpec((tm, LANES), lambda i, c: (i, 0)),
        ],
        out_shape=[
            jax.ShapeDtypeStruct((t, GMLP_WIDTH), BF16),
            jax.ShapeDtypeStruct((t, CONV_CH), BF16),
            jax.ShapeDtypeStruct((t, LANES), F32),
            jax.ShapeDtypeStruct((t, LANES), F32),
        ],
        scratch_shapes=[
            pltpu.VMEM((tm, D_MODEL), BF16),
            pltpu.VMEM((n_groups, SUBLANES, cw), F32),
        ],
        compiler_params=pltpu.CompilerParams(
            dimension_semantics=("arbitrary", "arbitrary"),
            vmem_limit_bytes=VMEM_LIMIT),
    )(x2, gmix, w_in, w_in, w_in, w_in, w_in, vng, ws, bsb, conv_taps)


def _lane_cumsum(v, lane):
    shift = 1
    while shift < N_EXPERTS:
        v = v + jnp.where(lane >= shift, pltpu.roll(v, shift, axis=1), 0.0)
        shift *= 2
    return v


def _lane_dense(col_values, lane):
    tile = jnp.zeros(lane.shape, F32)
    for k, col in enumerate(col_values):
        tile = jnp.where(lane == k, col, tile)
    return tile.T[0:SUBLANES, :]


def _finish_pages(n_page_rows, cnt_ref, ptab_ref, xs_ref, tab_ref, zero_scr, sem):
    page_shift = EXP_ROWS.bit_length() - 1
    n_slots = tab_ref.shape[1]
    zero_scr[...] = jnp.zeros_like(zero_scr)

    def clear(s, carry):
        for row in range(TAB_ROWS):
            tab_ref[row, s] = 0
        return carry

    lax.fori_loop(0, n_slots, clear, 0)

    def pages_of(e):
        return (cnt_ref[0, e] + (EXP_ROWS - 1)) >> page_shift

    def list_pages(e, carry):
        first_slot, run = carry

        def put(j, carry):
            tab_ref[TAB_ORDER, first_slot + j] = ptab_ref[e, j]
            tab_ref[TAB_EXPERT, first_slot + j] = e
            tab_ref[TAB_WSLOT, first_slot + j] = run & 1
            return carry

        n_pages_e = pages_of(e)
        lax.fori_loop(0, n_pages_e, put, 0)

        @pl.when(n_pages_e > 0)
        def _():
            tab_ref[TAB_RUN_LEN, first_slot] = n_pages_e

        return first_slot + n_pages_e, run + jnp.where(n_pages_e > 0, 1, 0)

    n_pages, _ = lax.fori_loop(0, N_EXPERTS, list_pages, (0, 0))
    tab_ref[TAB_N_PAGES, 0] = n_pages
    last_page = tab_ref[TAB_ORDER, n_pages - 1]
    last_expert = tab_ref[TAB_EXPERT, n_pages - 1]

    def repeat_last(s, carry):
        tab_ref[TAB_ORDER, s] = last_page
        tab_ref[TAB_EXPERT, s] = last_expert
        return carry

    lax.fori_loop(n_pages, n_slots, repeat_last, 0)

    def pad_copies(e):
        used = cnt_ref[0, e] & (EXP_ROWS - 1)
        last = ptab_ref[e, jnp.maximum(pages_of(e) - 1, 0)]
        return _zero_range_copies(zero_scr, xs_ref, sem, last * EXP_ROWS + used,
                                  (EXP_ROWS - used) & (EXP_ROWS - 1))

    def tail_copies(p):
        return [
            pltpu.make_async_copy(
                zero_scr,
                xs_ref.at[pl.ds(pl.multiple_of(p * EXP_ROWS + part * ZERO_ROWS, ZERO_ROWS),
                                ZERO_ROWS)],
                sem)
            for part in range(EXP_ROWS // ZERO_ROWS)
        ]

    def start_pads(e, carry):
        for needed, copy in pad_copies(e):
            pl.when(needed)(copy.start)
        return carry

    def wait_pads(e, carry):
        for needed, copy in pad_copies(e):
            pl.when(needed)(copy.wait)
        return carry

    def start_tail(p, carry):
        for copy in tail_copies(p):
            copy.start()
        return carry

    def wait_tail(p, carry):
        for copy in tail_copies(p):
            copy.wait()
        return carry

    total_pages = n_page_rows // EXP_ROWS
    lax.fori_loop(0, N_EXPERTS, start_pads, 0)
    lax.fori_loop(n_pages, total_pages, start_tail, 0)
    lax.fori_loop(0, N_EXPERTS, wait_pads, 0)
    lax.fori_loop(n_pages, total_pages, wait_tail, 0)


def _out_route_body(trash_row0, ya_ref, yb_ref, sa_ref, sb_ref, x_ref, ga_ref, gb_ref, wout_ref,
                    gffn_ref, wr_ref, br_ref, h_ref, xs_ref, tab_ref,
                    row_buf, pos_vmem, pos_smem, st_scr, ptab_scr, cnt_vmem, ptab_vmem,
                    cnt_smem, ptab_smem, zero_scr, row_sem, pos_sem, fill_sem):
    i = pl.program_id(0)
    n_steps = pl.num_programs(0) - 1
    tm = x_ref.shape[0]
    slot = lax.rem(i, 2)
    prev = 1 - slot

    def row_copy(r, k, prev=prev):
        return pltpu.make_async_copy(
            row_buf.at[prev, pl.ds(r, 1)], xs_ref.at[pl.ds(pos_smem[prev, k, r], 1)], row_sem)

    def wait_step_rows():
        for _ in range(2):
            pltpu.make_async_copy(row_buf.at[0], xs_ref.at[pl.ds(0, tm)], row_sem).wait()

    def pos_copy(s):
        return pltpu.make_async_copy(pos_vmem, pos_smem.at[s], pos_sem)

    @pl.when(i == 0)
    def _():
        st_scr[...] = jnp.zeros_like(st_scr)
        ptab_scr[...] = jnp.zeros_like(ptab_scr)
        row_buf[1] = jnp.zeros((tm, XS_COLS), F32)
        which = lax.broadcasted_iota(I32, (SUBLANES, tm), 0)
        pos_vmem[...] = (trash_row0 + lax.broadcasted_iota(I32, (SUBLANES, tm), 1)
                         + jnp.where(which == 1, tm, 0))
        pos_copy(1).start()

    @pl.when(i > 0)
    def _():
        wait_step_rows()

    def route_block(slot):
        prev = 1 - slot
        rows_per_chunk = tm // (D_MODEL // OUT_CHUNK)

        ra = _rms_scale(jnp.sum(sa_ref[...], axis=-1, keepdims=True), GMLP_WIDTH)
        rb = _rms_scale(jnp.sum(sb_ref[...], axis=-1, keepdims=True), CONV_CH)
        yna = (ya_ref[...].astype(F32) * ra * ga_ref[...]).astype(BF16)
        ynb = (yb_ref[...].astype(F32) * rb * gb_ref[...]).astype(BF16)
        yn = jnp.concatenate([yna, ynb], axis=1)
        for j in range(D_MODEL // OUT_CHUNK):
            cols = slice(j * OUT_CHUNK, (j + 1) * OUT_CHUNK)
            h_ref[:, cols] = x_ref[:, cols] + jnp.dot(
                yn, wout_ref[:, cols], preferred_element_type=F32)
            if j == 0:
                pos_copy(prev).wait()
            for r in range(j * rows_per_chunk, (j + 1) * rows_per_chunk):
                for k in range(2):
                    row_copy(r, k, prev).start(priority=k)
        h = h_ref[...]

        ms = jnp.mean(h * h, axis=-1, keepdims=True)
        hn = h * lax.rsqrt(ms + EPS) * gffn_ref[...]
        logits = jnp.dot(hn.astype(BF16), wr_ref[...], preferred_element_type=F32) + br_ref[...]

        lane = lax.broadcasted_iota(I32, (tm, LANES), 1)
        lane_f = lane.astype(F32)
        big = float(LANES)

        is_g = (lane >= GROUP_LANE0) & (lane < GROUP_LANE0 + N_GROUPS)
        lg = jnp.where(is_g, logits, MASKED)
        lg_top = jnp.max(lg, axis=-1, keepdims=True)
        g_idx = jnp.min(jnp.where(is_g & (lg == lg_top), lane_f - GROUP_LANE0, big),
                        axis=-1, keepdims=True).astype(I32)
        pg_top = 1.0 / jnp.sum(jnp.where(is_g, jnp.exp(lg - lg_top), 0.0), axis=-1, keepdims=True)

        group_shift = EXPERTS_PER_GROUP.bit_length() - 1
        in_grp = (lane < N_EXPERTS) & ((lane >> group_shift) == g_idx)
        le = jnp.where(in_grp, logits, MASKED)
        le1 = jnp.max(le, axis=-1, keepdims=True)
        e1 = jnp.min(jnp.where(in_grp & (le == le1), lane_f, big), axis=-1, keepdims=True)
        denom = jnp.sum(jnp.where(in_grp, jnp.exp(le - le1), 0.0), axis=-1, keepdims=True)
        rest = in_grp & (lane_f != e1)
        le2 = jnp.max(jnp.where(rest, le, MASKED), axis=-1, keepdims=True)
        e2 = jnp.min(jnp.where(rest & (le == le2), lane_f, big), axis=-1, keepdims=True)
        q1 = 1.0 / denom
        q2 = jnp.exp(le2 - le1) / denom
        qs = q1 + q2
        gate1 = pg_top * (q1 / qs)
        gate2 = pg_top * (q2 / qs)

        is1 = lane_f == e1
        is2 = lane_f == e2
        sel = jnp.where(is1 | is2, 1.0, 0.0)
        t_i = lax.broadcasted_iota(I32, (tm, tm), 0)
        t_j = lax.broadcasted_iota(I32, (tm, tm), 1)
        before = jnp.where(t_j < t_i, 1.0, 0.0).astype(BF16)
        run0 = st_scr[0:1, :]
        cur_page = st_scr[1:2, :]
        next_free = st_scr[2:3, :]
        cum = jnp.dot(before, sel.astype(BF16), preferred_element_type=F32) + run0

        lane_row = lax.broadcasted_iota(I32, (1, LANES), 1)
        run1 = run0 + jnp.sum(sel, axis=0, keepdims=True)
        pages0 = jnp.ceil(run0 / EXP_ROWS)
        n_new = jnp.ceil(run1 / EXP_ROWS) - pages0
        first_new = next_free + _lane_cumsum(n_new, lane_row) - n_new

        def pick(mask, per_expert):
            return jnp.sum(jnp.where(mask, per_expert, 0.0), axis=-1, keepdims=True)

        def place(mask):
            rank = pick(mask, cum)
            page_idx = jnp.floor(rank / EXP_ROWS)
            owned = pick(mask, pages0)
            page = jnp.where(page_idx < owned, pick(mask, cur_page),
                             pick(mask, first_new) + (page_idx - owned))
            return page * EXP_ROWS + (rank - page_idx * EXP_ROWS)

        pos1 = place(is1)
        pos2 = place(is2)

        st_scr[0:1, :] = run1
        st_scr[1:2, :] = jnp.where(n_new > 0, first_new + n_new - 1, cur_page)
        st_scr[2:3, :] = next_free + jnp.sum(n_new, axis=-1, keepdims=True)

        per_expert = jnp.concatenate(
            [pages0, n_new, first_new, jnp.zeros((LANES - 3, LANES), F32)], axis=0).T
        owned_c = per_expert[0:N_EXPERTS, 0:1]
        n_new_c = per_expert[0:N_EXPERTS, 1:2]
        first_c = per_expert[0:N_EXPERTS, 2:3]
        idx = lax.broadcasted_iota(I32, (N_EXPERTS, LANES), 1).astype(F32)
        fresh = (idx >= owned_c) & (idx < owned_c + n_new_c)
        ptab_scr[...] = jnp.where(fresh, first_c + (idx - owned_c), ptab_scr[...])
        cnt_vmem[...] = jnp.broadcast_to(run1, cnt_vmem.shape).astype(I32)
        ptab_vmem[...] = ptab_scr[...].astype(I32)

        token = (i * tm + lax.broadcasted_iota(I32, (tm, 1), 0) + 1).astype(F32)
        meta = jnp.zeros((tm, LANES), F32)
        for k, col in ((META_TOKEN, token), (META_GATE0, gate1), (META_POS0, pos1),
                       (META_GATE1, gate2), (META_POS1, pos2)):
            meta = jnp.where(lane == k, col, meta)
        row_buf[slot, :, 0:D_MODEL] = hn
        row_buf[slot, :, D_MODEL:XS_COLS] = meta
        pos_vmem[...] = _lane_dense([pos1, pos2], lane).astype(I32)
        pos_copy(slot).start()

    for static_slot in range(2):
        pl.when((i < n_steps) & (slot == static_slot))(
            functools.partial(route_block, static_slot))

    @pl.when(i == n_steps)
    def _():
        def issue(r, carry):
            for k in range(2):
                row_copy(r, k).start(priority=k)
            return carry

        pos_copy(prev).wait()
        lax.fori_loop(0, tm, issue, 0, unroll=8)
        wait_step_rows()

        for vec, scal in ((cnt_vmem, cnt_smem), (ptab_vmem, ptab_smem)):
            copy = pltpu.make_async_copy(vec, scal, fill_sem)
            copy.start()
            copy.wait()
        _finish_pages(trash_row0, cnt_smem, ptab_smem, xs_ref, tab_ref, zero_scr, fill_sem)


def _out_route(ya, yb, sa, sb, x2, ga, gb, w_out, gffn, wr, br, n_page_rows):
    t = x2.shape[0]
    tm = OUT_ROWS
    n_steps = t // tm
    const = lambda i: (0, 0)
    rows = lambda i: (jnp.minimum(i, n_steps - 1), 0)
    return pl.pallas_call(
        functools.partial(_out_route_body, n_page_rows),
        name="out_route",
        grid=(n_steps + 1,),
        in_specs=[
            pl.BlockSpec((tm, GMLP_WIDTH), rows),
            pl.BlockSpec((tm, CONV_CH), rows),
            pl.BlockSpec((tm, LANES), rows),
            pl.BlockSpec((tm, LANES), rows),
            pl.BlockSpec((tm, D_MODEL), rows),
            pl.BlockSpec((1, GMLP_WIDTH), const),
            pl.BlockSpec((1, CONV_CH), const),
            pl.BlockSpec((D_MODEL, D_MODEL), const, pipeline_mode=pl.Buffered(1)),
            pl.BlockSpec((1, D_MODEL), const),
            pl.BlockSpec((D_MODEL, LANES), const),
            pl.BlockSpec((1, LANES), const),
        ],
        out_specs=[
            pl.BlockSpec((tm, D_MODEL), rows),
            pl.BlockSpec(memory_space=pl.ANY),
            pl.BlockSpec(memory_space=pltpu.SMEM),
        ],
        out_shape=[
            jax.ShapeDtypeStruct((t, D_MODEL), F32),
            jax.ShapeDtypeStruct((n_page_rows + 2 * tm, XS_COLS), F32),
            jax.ShapeDtypeStruct((TAB_ROWS, TAB_LANES), I32),
        ],
        scratch_shapes=[
            pltpu.VMEM((2, tm, XS_COLS), F32),
            pltpu.VMEM((SUBLANES, tm), I32),
            pltpu.SMEM((2, SUBLANES, tm), I32),
            pltpu.VMEM((SUBLANES, LANES), F32),
            pltpu.VMEM((N_EXPERTS, LANES), F32),
            pltpu.VMEM((SUBLANES, LANES), I32),
            pltpu.VMEM((N_EXPERTS, LANES), I32),
            pltpu.SMEM((SUBLANES, LANES), I32),
            pltpu.SMEM((N_EXPERTS, LANES), I32),
            pltpu.VMEM((ZERO_ROWS, XS_COLS), F32),
            pltpu.SemaphoreType.DMA,
            pltpu.SemaphoreType.DMA,
            pltpu.SemaphoreType.DMA,
        ],
        compiler_params=pltpu.CompilerParams(
            dimension_semantics=("arbitrary",),
            vmem_limit_bytes=VMEM_LIMIT),
    )(ya, yb, sa, sb, x2, ga, gb, w_out, gffn, wr, br)


def _zero_range_copies(zero_scr, xs_ref, sem, start, length):
    head = length & (SUBLANES - 1)
    copies = []
    for j in range(SUBLANES - 1):
        copy = pltpu.make_async_copy(
            zero_scr.at[pl.ds(0, 1)], xs_ref.at[pl.ds(start + j, 1)], sem)
        copies.append((j < head, copy))
    body_start = start + head
    body = length - head
    size = ZERO_ROWS
    while size >= SUBLANES:
        offset = pl.multiple_of(body_start + (body & ~(2 * size - 1)), SUBLANES)
        copy = pltpu.make_async_copy(
            zero_scr.at[pl.ds(0, size)], xs_ref.at[pl.ds(offset, size)], sem)
        copies.append(((body & size) != 0, copy))
        size //= 2
    return copies


SLOTS_PER_STEP = 3


def _expert_mlp_body(n_tokens, tab_ref, *refs):
    page_refs, rest = refs[:SLOTS_PER_STEP], refs[SLOTS_PER_STEP:]
    for j, xs_ref in enumerate(page_refs):
        _expert_slot(n_tokens, SLOTS_PER_STEP * pl.program_id(0) + j, tab_ref, xs_ref, *rest)


def _expert_slot(n_tokens, s, tab_ref, xs_ref, wg_hbm, wu_hbm, wd_hbm, out2_ref,
                 wg_f32, wu_f32, wd_f32, wg_scr, wu_scr, wd_scr, out_buf, dest_vmem,
                 dest_smem, row_sem, dest_sem, w_sem):
    n_pages = tab_ref[TAB_N_PAGES, 0]
    bm = xs_ref.shape[0]
    cur = lax.rem(s, 3)
    prv = lax.rem(s + 2, 3)
    par = lax.rem(s, 2)
    ppar = 1 - par
    trash_row0 = 2 * n_tokens
    expert = tab_ref[TAB_EXPERT, s]
    run_len = tab_ref[TAB_RUN_LEN, s]
    wslot = tab_ref[TAB_WSLOT, s]

    def weight_copies(e, slot):
        return [
            pltpu.make_async_copy(src.at[e], dst.at[slot], w_sem.at[slot])
            for src, dst in ((wg_hbm, wg_f32), (wu_hbm, wu_f32), (wd_hbm, wd_f32))
        ]

    def row_copy(r, prv=prv):
        return pltpu.make_async_copy(
            out_buf.at[prv, pl.ds(r, 1)], out2_ref.at[pl.ds(dest_smem[ppar, 0, r], 1)],
            row_sem.at[prv])

    def wait_rows(slot):
        pltpu.make_async_copy(out_buf.at[0], out2_ref.at[pl.ds(0, bm)], row_sem.at[slot]).wait()

    def dest_copy(p):
        return pltpu.make_async_copy(dest_vmem, dest_smem.at[p], dest_sem)

    @pl.when(s == 0)
    def _():
        out_buf[2] = jnp.zeros((bm, D_MODEL), F32)
        for half in range(2):
            clear = pltpu.make_async_copy(
                out_buf.at[2], out2_ref.at[pl.ds(trash_row0 + half * bm, bm)], dest_sem)
            clear.start()
            clear.wait()

        def fill(r, carry):
            dest_smem[1, 0, r] = trash_row0 + bm + r
            return carry

        lax.fori_loop(0, bm, fill, 0)
        for copy in weight_copies(expert, wslot):
            copy.start()

    @pl.when((s >= 2) & (s <= n_pages))
    def _():
        wait_rows(cur)

    @pl.when((s >= 1) & (s <= n_pages))
    def _():
        dest_copy(ppar).wait()

    @pl.when((s < n_pages) & (run_len > 0))
    def _():
        for copy in weight_copies(expert, wslot):
            copy.wait()
        wg_scr[...] = wg_f32[wslot].astype(BF16)
        wu_scr[...] = wu_f32[wslot].astype(BF16)
        wd_scr[...] = wd_f32[wslot].astype(BF16)
        next_run = s + run_len

        @pl.when(next_run < n_pages)
        def _():
            for copy in weight_copies(tab_ref[TAB_EXPERT, next_run], 1 - wslot):
                copy.start()

    def compute_page(cur):
        prv = (cur + 2) % 3
        down_parts = 2
        n_groups = 2 + down_parts
        group_rows = bm // n_groups
        part_d = D_MODEL // down_parts

        def scatter_group(g):
            for r in range(g * group_rows, (g + 1) * group_rows):
                row_copy(r, prv).start(priority=r % 2)

        x = xs_ref[:, 0:D_MODEL].astype(BF16)
        gate = jnp.dot(x, wg_scr[...], preferred_element_type=F32)

        meta = xs_ref[:, D_MODEL:XS_COLS]
        own_row = (tab_ref[TAB_ORDER, s] * bm
                   + lax.broadcasted_iota(I32, (bm, 1), 0)).astype(F32)
        local = lax.broadcasted_iota(I32, (bm, 1), 0).astype(F32)
        token = meta[:, META_TOKEN:META_TOKEN + 1]
        first = meta[:, META_POS0:META_POS0 + 1] == own_row
        weight = jnp.where(first, meta[:, META_GATE0:META_GATE0 + 1],
                           meta[:, META_GATE1:META_GATE1 + 1])
        plane = jnp.where(first, 0.0, float(n_tokens))
        dest = jnp.where(token > 0.0, plane + token - 1.0,
                         (trash_row0 + par * bm).astype(F32) + local)
        lane = lax.broadcasted_iota(I32, (bm, LANES), 1)
        dest_vmem[...] = _lane_dense([dest], lane).astype(I32)
        dest_copy(par).start()
        scatter_group(0)
        up = jnp.dot(x, wu_scr[...], preferred_element_type=F32)
        scatter_group(1)
        hidden = (jax.nn.silu(gate) * up).astype(BF16)

        for part in range(down_parts):
            cols = slice(part * part_d, (part + 1) * part_d)
            out_buf[cur, :, cols] = jnp.dot(
                hidden, wd_scr[:, cols], preferred_element_type=F32) * weight
            scatter_group(2 + part)

    for static_cur in range(3):
        pl.when((s < n_pages) & (cur == static_cur))(
            functools.partial(compute_page, static_cur))

    @pl.when(s == n_pages)
    def _():
        def issue(pair, carry):
            for k in range(2):
                row_copy(2 * pair + k).start(priority=k)
            return carry

        lax.fori_loop(0, bm // 2, issue, 0, unroll=4)
        wait_rows(lax.rem(s + 1, 3))
        wait_rows(prv)


def _expert_mlp(tab, xs, w_gate, w_up, w_down, n_tokens, n_pages_max):
    bm = EXP_ROWS
    page_of = lambda j: pl.BlockSpec(
        (bm, XS_COLS), lambda g, tab: (tab[TAB_ORDER, SLOTS_PER_STEP * g + j], 0))
    n_slots = n_pages_max + 1
    assert n_slots + SLOTS_PER_STEP <= TAB_LANES
    return pl.pallas_call(
        functools.partial(_expert_mlp_body, n_tokens),
        name="expert_mlp",
        grid_spec=pltpu.PrefetchScalarGridSpec(
            num_scalar_prefetch=1,
            grid=(pl.cdiv(n_slots, SLOTS_PER_STEP),),
            in_specs=[
                *[page_of(j) for j in range(SLOTS_PER_STEP)],
                pl.BlockSpec(memory_space=pl.ANY),
                pl.BlockSpec(memory_space=pl.ANY),
                pl.BlockSpec(memory_space=pl.ANY),
            ],
            out_specs=pl.BlockSpec(memory_space=pl.ANY),
            scratch_shapes=[
                pltpu.VMEM((2, D_MODEL, D_EXPERT), F32),
                pltpu.VMEM((2, D_MODEL, D_EXPERT), F32),
                pltpu.VMEM((2, D_EXPERT, D_MODEL), F32),
                pltpu.VMEM((D_MODEL, D_EXPERT), BF16),
                pltpu.VMEM((D_MODEL, D_EXPERT), BF16),
                pltpu.VMEM((D_EXPERT, D_MODEL), BF16),
                pltpu.VMEM((3, bm, D_MODEL), F32),
                pltpu.VMEM((SUBLANES, bm), I32),
                pltpu.SMEM((2, SUBLANES, bm), I32),
                pltpu.SemaphoreType.DMA((3,)),
                pltpu.SemaphoreType.DMA,
                pltpu.SemaphoreType.DMA((2,)),
            ],
        ),
        out_shape=jax.ShapeDtypeStruct((2 * n_tokens + 2 * bm, D_MODEL), F32),
        compiler_params=pltpu.CompilerParams(
            dimension_semantics=("arbitrary",),
            vmem_limit_bytes=VMEM_LIMIT),
    )(tab, *([xs] * SLOTS_PER_STEP), w_gate, w_up, w_down)


def _final_norm_body(h_ref, y0_ref, y1_ref, gfin_ref, out_ref):
    h = h_ref[...] + (y0_ref[...] + y1_ref[...])
    ms = jnp.mean(h * h, axis=-1, keepdims=True)
    out_ref[...] = h * lax.rsqrt(ms + EPS) * gfin_ref[...]


def _final_norm(h, out2, gfin):
    t = h.shape[0]
    tm = FIN_ROWS
    plane1 = t // tm
    return pl.pallas_call(
        _final_norm_body,
        name="final_norm",
        grid=(t // tm,),
        in_specs=[
            pl.BlockSpec((tm, D_MODEL), lambda i: (i, 0)),
            pl.BlockSpec((tm, D_MODEL), lambda i: (i, 0)),
            pl.BlockSpec((tm, D_MODEL), lambda i: (i + plane1, 0)),
            pl.BlockSpec((1, D_MODEL), lambda i: (0, 0)),
        ],
        out_specs=pl.BlockSpec((tm, D_MODEL), lambda i: (i, 0)),
        out_shape=jax.ShapeDtypeStruct((t, D_MODEL), F32),
        compiler_params=pltpu.CompilerParams(
            dimension_semantics=("arbitrary",),
            vmem_limit_bytes=VMEM_LIMIT),
    )(h, out2, out2, gfin)


def kernel(x, norm_mix_g, w_in, gmlp_v_norm_g, gmlp_ws, gmlp_bs, conv_w, out_norm_gmlp_g,
           out_norm_conv_g, w_out, norm_ffn_g, router_group_w, router_group_b, router_expert_w,
           router_expert_b, expert_w_gate, expert_w_up, expert_w_down, norm_final_g):
    batch, seq_len, d_model = x.shape
    t = batch * seq_len
    assert w_in.shape[0] == 1, "single-layer block"
    assert t // EXP_ROWS <= LANES, "page table holds at most LANES pages per expert"
    n_col_groups = GMLP_WIDTH // MIX_COLS
    n_pages_max = 2 * t // EXP_ROWS + N_EXPERTS
    n_page_rows = n_pages_max * EXP_ROWS
    x2 = x.reshape(t, d_model)

    vng = gmlp_v_norm_g[0].reshape(n_col_groups, 1, MIX_COLS)
    bsb = jnp.broadcast_to(gmlp_bs[0][:, :, None], (GMLP_HEADS, GMLP_BLOCK, HEAD_DIM))
    taps = conv_w[0].reshape(CONV_K, n_col_groups, MIX_COLS).transpose(1, 0, 2)
    unused = LANES - N_EXPERTS - N_GROUPS
    wr = jnp.concatenate(
        [router_expert_w[0], router_group_w[0], jnp.zeros((d_model, unused), F32)],
        axis=1).astype(BF16)
    br = jnp.concatenate(
        [router_expert_b[0], router_group_b[0], jnp.zeros((unused,), F32)])[None, :]

    ya, yb, sa, sb = _mixer_proj(x2, norm_mix_g[0][None, :], w_in[0].astype(BF16), vng,
                                 gmlp_ws[0], bsb, taps, seq_len)
    h, xs, tab = _out_route(
        ya, yb, sa, sb, x2, out_norm_gmlp_g[0][None, :], out_norm_conv_g[0][None, :],
        w_out[0].astype(BF16), norm_ffn_g[0][None, :], wr, br, n_page_rows)
    out2 = _expert_mlp(tab, xs, expert_w_gate[0], expert_w_up[0], expert_w_down[0],
                       t, n_pages_max)
    out = _final_norm(h, out2, norm_final_g[None, :])
    return out.reshape(batch, seq_len, d_model)
```

```python
import functools

import jax
import jax.numpy as jnp
from jax import lax
from jax.experimental import pallas as pl
from jax.experimental.pallas import tpu as pltpu

F32 = jnp.float32
BF16 = jnp.bfloat16
I32 = jnp.int32

D_MODEL = 2048
CHUNK = 64
GMLP_WIDTH = 1024
GMLP_HEADS = 8
HEAD_DIM = 128
GMLP_BLOCK = 128
CONV_CH = 1024
CONV_K = 3
N_GROUPS = 4
EXPERTS_PER_GROUP = 8
N_EXPERTS = 32
D_EXPERT = 512
EPS = 1e-6

LANES = 128
SUBLANES = 8

MIX_ROWS = 1024
MIX_COLS = 256
OUT_ROWS = 512
OUT_CHUNK = 512
EXP_ROWS = 256
FIN_ROWS = 512
ZERO_ROWS = 128
GROUP_LANE0 = N_EXPERTS
MASKED = -1e30
VMEM_LIMIT = 56 * 1024 * 1024

XS_COLS = D_MODEL + LANES
META_TOKEN = 0
META_GATE0 = 1
META_POS0 = 2
META_GATE1 = 3
META_POS1 = 4

TAB_ORDER = 0
TAB_EXPERT = 1
TAB_N_PAGES = 2
TAB_RUN_LEN = 3
TAB_WSLOT = 4
TAB_ROWS = 8
TAB_LANES = 256


def _rms_scale(sumsq, width):
    return lax.rsqrt(sumsq / width + EPS)


def _mixer_proj_body(seq_len, x_ref, gmix_ref, wu_ref, wv_ref, wbg_ref, wcg_ref, whv_ref,
                     vng_ref, ws_ref, bsb_ref, cw_ref,
                     ya_ref, yb_ref, sa_ref, sb_ref, xn_scr, carry_scr):
    i = pl.program_id(0)
    c = pl.program_id(1)
    tm = x_ref.shape[0]
    cw = ya_ref.shape[1]
    heads_per_step = cw // HEAD_DIM

    @pl.when(c == 0)
    def _():
        x = x_ref[...]
        ms = jnp.mean(x * x, axis=-1, keepdims=True)
        xn_scr[...] = (x * lax.rsqrt(ms + EPS) * gmix_ref[...]).astype(BF16)
        sa_ref[...] = jnp.zeros_like(sa_ref)
        sb_ref[...] = jnp.zeros_like(sb_ref)

    xn = xn_scr[...]
    project = lambda w_ref: jnp.dot(xn, w_ref[...], preferred_element_type=F32)
    u = jax.nn.gelu(project(wu_ref))
    v = jax.nn.gelu(project(wv_ref))
    bg = project(wbg_ref)
    cg = project(wcg_ref)
    hv = project(whv_ref)

    pos_i = lax.broadcasted_iota(I32, (GMLP_BLOCK, GMLP_BLOCK), 0)
    pos_j = lax.broadcasted_iota(I32, (GMLP_BLOCK, GMLP_BLOCK), 1)
    chunk_shift = CHUNK.bit_length() - 1
    causal = (pos_i >> chunk_shift) >= (pos_j >> chunk_shift)
    vng = vng_ref[0]
    ya_heads = []
    for j in range(heads_per_step):
        head = c * heads_per_step + j
        lanes = slice(j * HEAD_DIM, (j + 1) * HEAD_DIM)
        vj = v[:, lanes]
        ms = jnp.mean(vj * vj, axis=-1, keepdims=True)
        vn = (vj * lax.rsqrt(ms + EPS) * vng[:, lanes]).astype(BF16)
        w_mix = jnp.where(causal, ws_ref[head], 0.0).astype(BF16)
        bias = bsb_ref[head]
        mixed = [
            jnp.dot(w_mix, vn[p * GMLP_BLOCK:(p + 1) * GMLP_BLOCK, :],
                    preferred_element_type=F32) + bias
            for p in range(tm // GMLP_BLOCK)
        ]
        ya_heads.append(u[:, lanes] * jnp.concatenate(mixed, axis=0))
    ya = jnp.concatenate(ya_heads, axis=1) if heads_per_step > 1 else ya_heads[0]

    z = cg * hv
    prev = carry_scr[c]
    seq_start = (i * tm) % seq_len == 0
    prev = jnp.where(seq_start, 0.0, prev)
    row = lax.broadcasted_iota(I32, (tm, cw), 0)
    z1 = jnp.where(row == 0, prev[SUBLANES - 1:SUBLANES, :], pltpu.roll(z, 1, axis=0))
    z2 = jnp.where(row == 0, prev[SUBLANES - 2:SUBLANES - 1, :],
                   jnp.where(row == 1, prev[SUBLANES - 1:SUBLANES, :],
                             pltpu.roll(z, 2, axis=0)))
    taps = cw_ref[0]
    conv = taps[0:1, :] * z2 + taps[1:2, :] * z1 + taps[2:3, :] * z
    yb = bg * conv
    carry_scr[c] = z[tm - SUBLANES:tm, :]

    ya_ref[...] = ya.astype(BF16)
    yb_ref[...] = yb.astype(BF16)
    ya2 = ya * ya
    yb2 = yb * yb
    sa_ref[...] += sum(ya2[:, k * LANES:(k + 1) * LANES] for k in range(cw // LANES))
    sb_ref[...] += sum(yb2[:, k * LANES:(k + 1) * LANES] for k in range(cw // LANES))


def _mixer_proj(x2, gmix, w_in, vng, ws, bsb, conv_taps, seq_len):
    t = x2.shape[0]
    cw = MIX_COLS
    tm = MIX_ROWS
    n_groups = GMLP_WIDTH // cw
    grid = (t // tm, n_groups)
    w_part = lambda k: pl.BlockSpec((D_MODEL, cw), lambda i, c: (0, k * n_groups + c))
    return pl.pallas_call(
        functools.partial(_mixer_proj_body, seq_len),
        name="mixer_proj",
        grid=grid,
        in_specs=[
            pl.BlockSpec((tm, D_MODEL), lambda i, c: (i, 0)),
            pl.BlockSpec((1, D_MODEL), lambda i, c: (0, 0)),
            w_part(0), w_part(1), w_part(2), w_part(3), w_part(4),
            pl.BlockSpec((1, 1, cw), lambda i, c: (c, 0, 0)),
            pl.BlockSpec((GMLP_HEADS, GMLP_BLOCK, GMLP_BLOCK), lambda i, c: (0, 0, 0)),
            pl.BlockSpec((GMLP_HEADS, GMLP_BLOCK, HEAD_DIM), lambda i, c: (0, 0, 0)),
            pl.BlockSpec((1, CONV_K, cw), lambda i, c: (c, 0, 0)),
        ],
        out_specs=[
            pl.BlockSpec((tm, cw), lambda i, c: (i, c)),
            pl.BlockSpec((tm, cw), lambda i, c: (i, c)),
            pl.BlockSpec((tm, LANES), lambda i, c: (i, 0)),
            pl.BlockSpec((tm, LANES), lambda i, c: (i, 0)),
        ],
        out_shape=[
            jax.ShapeDtypeStruct((t, GMLP_WIDTH), BF16),
            jax.ShapeDtypeStruct((t, CONV_CH), BF16),
            jax.ShapeDtypeStruct((t, LANES), F32),
            jax.ShapeDtypeStruct((t, LANES), F32),
        ],
        scratch_shapes=[
            pltpu.VMEM((tm, D_MODEL), BF16),
            pltpu.VMEM((n_groups, SUBLANES, cw), F32),
        ],
        compiler_params=pltpu.CompilerParams(
            dimension_semantics=("arbitrary", "arbitrary"),
            vmem_limit_bytes=VMEM_LIMIT),
    )(x2, gmix, w_in, w_in, w_in, w_in, w_in, vng, ws, bsb, conv_taps)


def _lane_cumsum(v, lane):
    shift = 1
    while shift < N_EXPERTS:
        v = v + jnp.where(lane >= shift, pltpu.roll(v, shift, axis=1), 0.0)
        shift *= 2
    return v


def _lane_dense(col_values, lane):
    tile = jnp.zeros(lane.shape, F32)
    for k, col in enumerate(col_values):
        tile = jnp.where(lane == k, col, tile)
    return tile.T[0:SUBLANES, :]


def _finish_pages(n_page_rows, cnt_ref, ptab_ref, xs_ref, tab_ref, zero_scr, sem):
    page_shift = EXP_ROWS.bit_length() - 1
    n_slots = tab_ref.shape[1]
    zero_scr[...] = jnp.zeros_like(zero_scr)

    def clear(s, carry):
        for row in range(TAB_ROWS):
            tab_ref[row, s] = 0
        return carry

    lax.fori_loop(0, n_slots, clear, 0)

    def pages_of(e):
        return (cnt_ref[0, e] + (EXP_ROWS - 1)) >> page_shift

    def list_pages(e, carry):
        first_slot, run = carry

        def put(j, carry):
            tab_ref[TAB_ORDER, first_slot + j] = ptab_ref[e, j]
            tab_ref[TAB_EXPERT, first_slot + j] = e
            tab_ref[TAB_WSLOT, first_slot + j] = run & 1
            return carry

        n_pages_e = pages_of(e)
        lax.fori_loop(0, n_pages_e, put, 0)

        @pl.when(n_pages_e > 0)
        def _():
            tab_ref[TAB_RUN_LEN, first_slot] = n_pages_e

        return first_slot + n_pages_e, run + jnp.where(n_pages_e > 0, 1, 0)

    n_pages, _ = lax.fori_loop(0, N_EXPERTS, list_pages, (0, 0))
    tab_ref[TAB_N_PAGES, 0] = n_pages
    last_page = tab_ref[TAB_ORDER, n_pages - 1]
    last_expert = tab_ref[TAB_EXPERT, n_pages - 1]

    def repeat_last(s, carry):
        tab_ref[TAB_ORDER, s] = last_page
        tab_ref[TAB_EXPERT, s] = last_expert
        return carry

    lax.fori_loop(n_pages, n_slots, repeat_last, 0)

    def pad_copies(e):
        used = cnt_ref[0, e] & (EXP_ROWS - 1)
        last = ptab_ref[e, jnp.maximum(pages_of(e) - 1, 0)]
        return _zero_range_copies(zero_scr, xs_ref, sem, last * EXP_ROWS + used,
                                  (EXP_ROWS - used) & (EXP_ROWS - 1))

    def tail_copies(p):
        return [
            pltpu.make_async_copy(
                zero_scr,
                xs_ref.at[pl.ds(pl.multiple_of(p * EXP_ROWS + part * ZERO_ROWS, ZERO_ROWS),
                                ZERO_ROWS)],
                sem)
            for part in range(EXP_ROWS // ZERO_ROWS)
        ]

    def start_pads(e, carry):
        for needed, copy in pad_copies(e):
            pl.when(needed)(copy.start)
        return carry

    def wait_pads(e, carry):
        for needed, copy in pad_copies(e):
            pl.when(needed)(copy.wait)
        return carry

    def start_tail(p, carry):
        for copy in tail_copies(p):
            copy.start()
        return carry

    def wait_tail(p, carry):
        for copy in tail_copies(p):
            copy.wait()
        return carry

    total_pages = n_page_rows // EXP_ROWS
    lax.fori_loop(0, N_EXPERTS, start_pads, 0)
    lax.fori_loop(n_pages, total_pages, start_tail, 0)
    lax.fori_loop(0, N_EXPERTS, wait_pads, 0)
    lax.fori_loop(n_pages, total_pages, wait_tail, 0)


def _out_route_body(trash_row0, ya_ref, yb_ref, sa_ref, sb_ref, x_ref, ga_ref, gb_ref, wout_ref,
                    gffn_ref, wr_ref, br_ref, h_ref, xs_ref, tab_ref,
                    row_buf, pos_vmem, pos_smem, st_scr, ptab_scr, cnt_vmem, ptab_vmem,
                    cnt_smem, ptab_smem, zero_scr, row_sem, pos_sem, fill_sem):
    i = pl.program_id(0)
    n_steps = pl.num_programs(0) - 1
    tm = x_ref.shape[0]
    slot = lax.rem(i, 2)
    prev = 1 - slot

    def row_copy(r, k, prev=prev):
        return pltpu.make_async_copy(
            row_buf.at[prev, pl.ds(r, 1)], xs_ref.at[pl.ds(pos_smem[prev, k, r], 1)], row_sem)

    def wait_step_rows():
        for _ in range(2):
            pltpu.make_async_copy(row_buf.at[0], xs_ref.at[pl.ds(0, tm)], row_sem).wait()

    def pos_copy(s):
        return pltpu.make_async_copy(pos_vmem, pos_smem.at[s], pos_sem)

    @pl.when(i == 0)
    def _():
        st_scr[...] = jnp.zeros_like(st_scr)
        ptab_scr[...] = jnp.zeros_like(ptab_scr)
        row_buf[1] = jnp.zeros((tm, XS_COLS), F32)
        which = lax.broadcasted_iota(I32, (SUBLANES, tm), 0)
        pos_vmem[...] = (trash_row0 + lax.broadcasted_iota(I32, (SUBLANES, tm), 1)
                         + jnp.where(which == 1, tm, 0))
        pos_copy(1).start()

    @pl.when(i > 0)
    def _():
        wait_step_rows()

    def route_block(slot):
        prev = 1 - slot
        rows_per_chunk = tm // (D_MODEL // OUT_CHUNK)

        ra = _rms_scale(jnp.sum(sa_ref[...], axis=-1, keepdims=True), GMLP_WIDTH)
        rb = _rms_scale(jnp.sum(sb_ref[...], axis=-1, keepdims=True), CONV_CH)
        yna = (ya_ref[...].astype(F32) * ra * ga_ref[...]).astype(BF16)
        ynb = (yb_ref[...].astype(F32) * rb * gb_ref[...]).astype(BF16)
        yn = jnp.concatenate([yna, ynb], axis=1)
        for j in range(D_MODEL // OUT_CHUNK):
            cols = slice(j * OUT_CHUNK, (j + 1) * OUT_CHUNK)
            h_ref[:, cols] = x_ref[:, cols] + jnp.dot(
                yn, wout_ref[:, cols], preferred_element_type=F32)
            if j == 0:
                pos_copy(prev).wait()
            for r in range(j * rows_per_chunk, (j + 1) * rows_per_chunk):
                for k in range(2):
                    row_copy(r, k, prev).start(priority=k)
        h = h_ref[...]

        ms = jnp.mean(h * h, axis=-1, keepdims=True)
        hn = h * lax.rsqrt(ms + EPS) * gffn_ref[...]
        logits = jnp.dot(hn.astype(BF16), wr_ref[...], preferred_element_type=F32) + br_ref[...]

        lane = lax.broadcasted_iota(I32, (tm, LANES), 1)
        lane_f = lane.astype(F32)
        big = float(LANES)

        is_g = (lane >= GROUP_LANE0) & (lane < GROUP_LANE0 + N_GROUPS)
        lg = jnp.where(is_g, logits, MASKED)
        lg_top = jnp.max(lg, axis=-1, keepdims=True)
        g_idx = jnp.min(jnp.where(is_g & (lg == lg_top), lane_f - GROUP_LANE0, big),
                        axis=-1, keepdims=True).astype(I32)
        pg_top = 1.0 / jnp.sum(jnp.where(is_g, jnp.exp(lg - lg_top), 0.0), axis=-1, keepdims=True)

        group_shift = EXPERTS_PER_GROUP.bit_length() - 1
        in_grp = (lane < N_EXPERTS) & ((lane >> group_shift) == g_idx)
        le = jnp.where(in_grp, logits, MASKED)
        le1 = jnp.max(le, axis=-1, keepdims=True)
        e1 = jnp.min(jnp.where(in_grp & (le == le1), lane_f, big), axis=-1, keepdims=True)
        denom = jnp.sum(jnp.where(in_grp, jnp.exp(le - le1), 0.0), axis=-1, keepdims=True)
        rest = in_grp & (lane_f != e1)
        le2 = jnp.max(jnp.where(rest, le, MASKED), axis=-1, keepdims=True)
        e2 = jnp.min(jnp.where(rest & (le == le2), lane_f, big), axis=-1, keepdims=True)
        q1 = 1.0 / denom
        q2 = jnp.exp(le2 - le1) / denom
        qs = q1 + q2
        gate1 = pg_top * (q1 / qs)
        gate2 = pg_top * (q2 / qs)

        is1 = lane_f == e1
        is2 = lane_f == e2
        sel = jnp.where(is1 | is2, 1.0, 0.0)
        t_i = lax.broadcasted_iota(I32, (tm, tm), 0)
        t_j = lax.broadcasted_iota(I32, (tm, tm), 1)
        before = jnp.where(t_j < t_i, 1.0, 0.0).astype(BF16)
        run0 = st_scr[0:1, :]
        cur_page = st_scr[1:2, :]
        next_free = st_scr[2:3, :]
        cum = jnp.dot(before, sel.astype(BF16), preferred_element_type=F32) + run0

        lane_row = lax.broadcasted_iota(I32, (1, LANES), 1)
        run1 = run0 + jnp.sum(sel, axis=0, keepdims=True)
        pages0 = jnp.ceil(run0 / EXP_ROWS)
        n_new = jnp.ceil(run1 / EXP_ROWS) - pages0
        first_new = next_free + _lane_cumsum(n_new, lane_row) - n_new

        def pick(mask, per_expert):
            return jnp.sum(jnp.where(mask, per_expert, 0.0), axis=-1, keepdims=True)

        def place(mask):
            rank = pick(mask, cum)
            page_idx = jnp.floor(rank / EXP_ROWS)
            owned = pick(mask, pages0)
            page = jnp.where(page_idx < owned, pick(mask, cur_page),
                             pick(mask, first_new) + (page_idx - owned))
            return page * EXP_ROWS + (rank - page_idx * EXP_ROWS)

        pos1 = place(is1)
        pos2 = place(is2)

        st_scr[0:1, :] = run1
        st_scr[1:2, :] = jnp.where(n_new > 0, first_new + n_new - 1, cur_page)
        st_scr[2:3, :] = next_free + jnp.sum(n_new, axis=-1, keepdims=True)

        per_expert = jnp.concatenate(
            [pages0, n_new, first_new, jnp.zeros((LANES - 3, LANES), F32)], axis=0).T
        owned_c = per_expert[0:N_EXPERTS, 0:1]
        n_new_c = per_expert[0:N_EXPERTS, 1:2]
        first_c = per_expert[0:N_EXPERTS, 2:3]
        idx = lax.broadcasted_iota(I32, (N_EXPERTS, LANES), 1).astype(F32)
        fresh = (idx >= owned_c) & (idx < owned_c + n_new_c)
        ptab_scr[...] = jnp.where(fresh, first_c + (idx - owned_c), ptab_scr[...])
        cnt_vmem[...] = jnp.broadcast_to(run1, cnt_vmem.shape).astype(I32)
        ptab_vmem[...] = ptab_scr[...].astype(I32)

        token = (i * tm + lax.broadcasted_iota(I32, (tm, 1), 0) + 1).astype(F32)
        meta = jnp.zeros((tm, LANES), F32)
        for k, col in ((META_TOKEN, token), (META_GATE0, gate1), (META_POS0, pos1),
                       (META_GATE1, gate2), (META_POS1, pos2)):
            meta = jnp.where(lane == k, col, meta)
        row_buf[slot, :, 0:D_MODEL] = hn
        row_buf[slot, :, D_MODEL:XS_COLS] = meta
        pos_vmem[...] = _lane_dense([pos1, pos2], lane).astype(I32)
        pos_copy(slot).start()

    for static_slot in range(2):
        pl.when((i < n_steps) & (slot == static_slot))(
            functools.partial(route_block, static_slot))

    @pl.when(i == n_steps)
    def _():
        def issue(r, carry):
            for k in range(2):
                row_copy(r, k).start(priority=k)
            return carry

        pos_copy(prev).wait()
        lax.fori_loop(0, tm, issue, 0, unroll=8)
        wait_step_rows()

        for vec, scal in ((cnt_vmem, cnt_smem), (ptab_vmem, ptab_smem)):
            copy = pltpu.make_async_copy(vec, scal, fill_sem)
            copy.start()
            copy.wait()
        _finish_pages(trash_row0, cnt_smem, ptab_smem, xs_ref, tab_ref, zero_scr, fill_sem)


def _out_route(ya, yb, sa, sb, x2, ga, gb, w_out, gffn, wr, br, n_page_rows):
    t = x2.shape[0]
    tm = OUT_ROWS
    n_steps = t // tm
    const = lambda i: (0, 0)
    rows = lambda i: (jnp.minimum(i, n_steps - 1), 0)
    return pl.pallas_call(
        functools.partial(_out_route_body, n_page_rows),
        name="out_route",
        grid=(n_steps + 1,),
        in_specs=[
            pl.BlockSpec((tm, GMLP_WIDTH), rows),
            pl.BlockSpec((tm, CONV_CH), rows),
            pl.BlockSpec((tm, LANES), rows),
            pl.BlockSpec((tm, LANES), rows),
            pl.BlockSpec((tm, D_MODEL), rows),
            pl.BlockSpec((1, GMLP_WIDTH), const),
            pl.BlockSpec((1, CONV_CH), const),
            pl.BlockSpec((D_MODEL, D_MODEL), const, pipeline_mode=pl.Buffered(1)),
            pl.BlockSpec((1, D_MODEL), const),
            pl.BlockSpec((D_MODEL, LANES), const),
            pl.BlockSpec((1, LANES), const),
        ],
        out_specs=[
            pl.BlockSpec((tm, D_MODEL), rows),
            pl.BlockSpec(memory_space=pl.ANY),
            pl.BlockSpec(memory_space=pltpu.SMEM),
        ],
        out_shape=[
            jax.ShapeDtypeStruct((t, D_MODEL), F32),
            jax.ShapeDtypeStruct((n_page_rows + 2 * tm, XS_COLS), F32),
            jax.ShapeDtypeStruct((TAB_ROWS, TAB_LANES), I32),
        ],
        scratch_shapes=[
            pltpu.VMEM((2, tm, XS_COLS), F32),
            pltpu.VMEM((SUBLANES, tm), I32),
            pltpu.SMEM((2, SUBLANES, tm), I32),
            pltpu.VMEM((SUBLANES, LANES), F32),
            pltpu.VMEM((N_EXPERTS, LANES), F32),
            pltpu.VMEM((SUBLANES, LANES), I32),
            pltpu.VMEM((N_EXPERTS, LANES), I32),
            pltpu.SMEM((SUBLANES, LANES), I32),
            pltpu.SMEM((N_EXPERTS, LANES), I32),
            pltpu.VMEM((ZERO_ROWS, XS_COLS), F32),
            pltpu.SemaphoreType.DMA,
            pltpu.SemaphoreType.DMA,
            pltpu.SemaphoreType.DMA,
        ],
        compiler_params=pltpu.CompilerParams(
            dimension_semantics=("arbitrary",),
            vmem_limit_bytes=VMEM_LIMIT,
            allow_input_fusion=[k == 7 for k in range(11)]),
    )(ya, yb, sa, sb, x2, ga, gb, w_out, gffn, wr, br)


def _zero_range_copies(zero_scr, xs_ref, sem, start, length):
    head = length & (SUBLANES - 1)
    copies = []
    for j in range(SUBLANES - 1):
        copy = pltpu.make_async_copy(
            zero_scr.at[pl.ds(0, 1)], xs_ref.at[pl.ds(start + j, 1)], sem)
        copies.append((j < head, copy))
    body_start = start + head
    body = length - head
    size = ZERO_ROWS
    while size >= SUBLANES:
        offset = pl.multiple_of(body_start + (body & ~(2 * size - 1)), SUBLANES)
        copy = pltpu.make_async_copy(
            zero_scr.at[pl.ds(0, size)], xs_ref.at[pl.ds(offset, size)], sem)
        copies.append(((body & size) != 0, copy))
        size //= 2
    return copies


SLOTS_PER_STEP = 2


def _expert_mlp_body(n_tokens, tab_ref, *refs):
    page_refs, rest = refs[:SLOTS_PER_STEP], refs[SLOTS_PER_STEP:]
    for j, xs_ref in enumerate(page_refs):
        _expert_slot(n_tokens, SLOTS_PER_STEP * pl.program_id(0) + j, tab_ref, xs_ref, *rest)


def _expert_slot(n_tokens, s, tab_ref, xs_ref, wg_hbm, wu_hbm, wd_hbm, out2_ref,
                 wg_f32, wu_f32, wd_f32, wg_scr, wu_scr, wd_scr, out_buf, dest_vmem,
                 dest_smem, row_sem, dest_sem, w_sem):
    n_pages = tab_ref[TAB_N_PAGES, 0]
    bm = xs_ref.shape[0]
    cur = lax.rem(s, 3)
    prv = lax.rem(s + 2, 3)
    par = lax.rem(s, 2)
    ppar = 1 - par
    trash_row0 = 2 * n_tokens
    expert = tab_ref[TAB_EXPERT, s]
    run_len = tab_ref[TAB_RUN_LEN, s]
    wslot = tab_ref[TAB_WSLOT, s]

    def weight_copies(e, slot):
        return [
            pltpu.make_async_copy(src.at[e], dst.at[slot], w_sem.at[slot])
            for src, dst in ((wg_hbm, wg_f32), (wu_hbm, wu_f32), (wd_hbm, wd_f32))
        ]

    def row_copy(r, prv=prv):
        return pltpu.make_async_copy(
            out_buf.at[prv, pl.ds(r, 1)], out2_ref.at[pl.ds(dest_smem[ppar, 0, r], 1)],
            row_sem.at[prv])

    def wait_rows(slot):
        pltpu.make_async_copy(out_buf.at[0], out2_ref.at[pl.ds(0, bm)], row_sem.at[slot]).wait()

    def dest_copy(p):
        return pltpu.make_async_copy(dest_vmem, dest_smem.at[p], dest_sem)

    @pl.when(s == 0)
    def _():
        out_buf[2] = jnp.zeros((bm, D_MODEL), F32)
        for half in range(2):
            clear = pltpu.make_async_copy(
                out_buf.at[2], out2_ref.at[pl.ds(trash_row0 + half * bm, bm)], dest_sem)
            clear.start()
            clear.wait()

        def fill(r, carry):
            dest_smem[1, 0, r] = trash_row0 + bm + r
            return carry

        lax.fori_loop(0, bm, fill, 0)
        for copy in weight_copies(expert, wslot):
            copy.start()

    @pl.when((s >= 2) & (s <= n_pages))
    def _():
        wait_rows(cur)

    @pl.when((s >= 1) & (s <= n_pages))
    def _():
        dest_copy(ppar).wait()

    @pl.when((s < n_pages) & (run_len > 0))
    def _():
        for copy in weight_copies(expert, wslot):
            copy.wait()
        wg_scr[...] = wg_f32[wslot].astype(BF16)
        wu_scr[...] = wu_f32[wslot].astype(BF16)
        wd_scr[...] = wd_f32[wslot].astype(BF16)
        next_run = s + run_len

        @pl.when(next_run < n_pages)
        def _():
            for copy in weight_copies(tab_ref[TAB_EXPERT, next_run], 1 - wslot):
                copy.start()

    def compute_page(cur):
        prv = (cur + 2) % 3
        down_parts = 2
        n_groups = 2 + down_parts
        group_rows = bm // n_groups
        part_d = D_MODEL // down_parts

        def scatter_group(g):
            for r in range(g * group_rows, (g + 1) * group_rows):
                row_copy(r, prv).start(priority=r % 2)

        x = xs_ref[:, 0:D_MODEL].astype(BF16)
        gate = jnp.dot(x, wg_scr[...], preferred_element_type=F32)

        meta = xs_ref[:, D_MODEL:XS_COLS]
        own_row = (tab_ref[TAB_ORDER, s] * bm
                   + lax.broadcasted_iota(I32, (bm, 1), 0)).astype(F32)
        local = lax.broadcasted_iota(I32, (bm, 1), 0).astype(F32)
        token = meta[:, META_TOKEN:META_TOKEN + 1]
        first = meta[:, META_POS0:META_POS0 + 1] == own_row
        weight = jnp.where(first, meta[:, META_GATE0:META_GATE0 + 1],
                           meta[:, META_GATE1:META_GATE1 + 1])
        plane = jnp.where(first, 0.0, float(n_tokens))
        dest = jnp.where(token > 0.0, plane + token - 1.0,
                         (trash_row0 + par * bm).astype(F32) + local)
        lane = lax.broadcasted_iota(I32, (bm, LANES), 1)
        dest_vmem[...] = _lane_dense([dest], lane).astype(I32)
        dest_copy(par).start()
        scatter_group(0)
        up = jnp.dot(x, wu_scr[...], preferred_element_type=F32)
        scatter_group(1)
        hidden = (jax.nn.silu(gate) * up).astype(BF16)

        for part in range(down_parts):
            cols = slice(part * part_d, (part + 1) * part_d)
            out_buf[cur, :, cols] = jnp.dot(
                hidden, wd_scr[:, cols], preferred_element_type=F32) * weight
            scatter_group(2 + part)

    for static_cur in range(3):
        pl.when((s < n_pages) & (cur == static_cur))(
            functools.partial(compute_page, static_cur))

    @pl.when(s == n_pages)
    def _():
        def issue(pair, carry):
            for k in range(2):
                row_copy(2 * pair + k).start(priority=k)
            return carry

        lax.fori_loop(0, bm // 2, issue, 0, unroll=4)
        wait_rows(lax.rem(s + 1, 3))
        wait_rows(prv)


def _expert_mlp(tab, xs, w_gate, w_up, w_down, n_tokens, n_pages_max):
    bm = EXP_ROWS
    page_of = lambda j: pl.BlockSpec(
        (bm, XS_COLS), lambda g, tab: (tab[TAB_ORDER, SLOTS_PER_STEP * g + j], 0))
    n_slots = n_pages_max + 1
    assert n_slots + SLOTS_PER_STEP <= TAB_LANES
    return pl.pallas_call(
        functools.partial(_expert_mlp_body, n_tokens),
        name="expert_mlp",
        grid_spec=pltpu.PrefetchScalarGridSpec(
            num_scalar_prefetch=1,
            grid=(pl.cdiv(n_slots, SLOTS_PER_STEP),),
            in_specs=[
                *[page_of(j) for j in range(SLOTS_PER_STEP)],
                pl.BlockSpec(memory_space=pl.ANY),
                pl.BlockSpec(memory_space=pl.ANY),
                pl.BlockSpec(memory_space=pl.ANY),
            ],
            out_specs=pl.BlockSpec(memory_space=pl.ANY),
            scratch_shapes=[
                pltpu.VMEM((2, D_MODEL, D_EXPERT), F32),
                pltpu.VMEM((2, D_MODEL, D_EXPERT), F32),
                pltpu.VMEM((2, D_EXPERT, D_MODEL), F32),
                pltpu.VMEM((D_MODEL, D_EXPERT), BF16),
                pltpu.VMEM((D_MODEL, D_EXPERT), BF16),
                pltpu.VMEM((D_EXPERT, D_MODEL), BF16),
                pltpu.VMEM((3, bm, D_MODEL), F32),
                pltpu.VMEM((SUBLANES, bm), I32),
                pltpu.SMEM((2, SUBLANES, bm), I32),
                pltpu.SemaphoreType.DMA((3,)),
                pltpu.SemaphoreType.DMA,
                pltpu.SemaphoreType.DMA((2,)),
            ],
        ),
        out_shape=jax.ShapeDtypeStruct((2 * n_tokens + 2 * bm, D_MODEL), F32),
        compiler_params=pltpu.CompilerParams(
            dimension_semantics=("arbitrary",),
            vmem_limit_bytes=VMEM_LIMIT),
    )(tab, *([xs] * SLOTS_PER_STEP), w_gate, w_up, w_down)


def _final_norm_body(h_ref, y0_ref, y1_ref, gfin_ref, out_ref):
    h = h_ref[...] + (y0_ref[...] + y1_ref[...])
    ms = jnp.mean(h * h, axis=-1, keepdims=True)
    out_ref[...] = h * lax.rsqrt(ms + EPS) * gfin_ref[...]


def _final_norm(h, out2, gfin):
    t = h.shape[0]
    tm = FIN_ROWS
    plane1 = t // tm
    return pl.pallas_call(
        _final_norm_body,
        name="final_norm",
        grid=(t // tm,),
        in_specs=[
            pl.BlockSpec((tm, D_MODEL), lambda i: (i, 0)),
            pl.BlockSpec((tm, D_MODEL), lambda i: (i, 0)),
            pl.BlockSpec((tm, D_MODEL), lambda i: (i + plane1, 0)),
            pl.BlockSpec((1, D_MODEL), lambda i: (0, 0)),
        ],
        out_specs=pl.BlockSpec((tm, D_MODEL), lambda i: (i, 0)),
        out_shape=jax.ShapeDtypeStruct((t, D_MODEL), F32),
        compiler_params=pltpu.CompilerParams(
            dimension_semantics=("arbitrary",),
            vmem_limit_bytes=VMEM_LIMIT),
    )(h, out2, out2, gfin)


def kernel(x, norm_mix_g, w_in, gmlp_v_norm_g, gmlp_ws, gmlp_bs, conv_w, out_norm_gmlp_g,
           out_norm_conv_g, w_out, norm_ffn_g, router_group_w, router_group_b, router_expert_w,
           router_expert_b, expert_w_gate, expert_w_up, expert_w_down, norm_final_g):
    batch, seq_len, d_model = x.shape
    t = batch * seq_len
    assert w_in.shape[0] == 1, "single-layer block"
    assert t // EXP_ROWS <= LANES, "page table holds at most LANES pages per expert"
    n_col_groups = GMLP_WIDTH // MIX_COLS
    n_pages_max = 2 * t // EXP_ROWS + N_EXPERTS
    n_page_rows = n_pages_max * EXP_ROWS
    x2 = x.reshape(t, d_model)

    vng = gmlp_v_norm_g[0].reshape(n_col_groups, 1, MIX_COLS)
    bsb = jnp.broadcast_to(gmlp_bs[0][:, :, None], (GMLP_HEADS, GMLP_BLOCK, HEAD_DIM))
    taps = conv_w[0].reshape(CONV_K, n_col_groups, MIX_COLS).transpose(1, 0, 2)
    unused = LANES - N_EXPERTS - N_GROUPS
    wr = jnp.concatenate(
        [router_expert_w[0], router_group_w[0], jnp.zeros((d_model, unused), F32)],
        axis=1).astype(BF16)
    br = jnp.concatenate(
        [router_expert_b[0], router_group_b[0], jnp.zeros((unused,), F32)])[None, :]

    ya, yb, sa, sb = _mixer_proj(x2, norm_mix_g[0][None, :], w_in[0].astype(BF16), vng,
                                 gmlp_ws[0], bsb, taps, seq_len)
    h, xs, tab = _out_route(
        ya, yb, sa, sb, x2, out_norm_gmlp_g[0][None, :], out_norm_conv_g[0][None, :],
        w_out[0].astype(BF16), norm_ffn_g[0][None, :], wr, br, n_page_rows)
    out2 = _expert_mlp(tab, xs, expert_w_gate[0], expert_w_up[0], expert_w_down[0],
                       t, n_pages_max)
    out = _final_norm(h, out2, norm_final_g[None, :])
    return out.reshape(batch, seq_len, d_model)
```
